```python
import math
import jax, jax.numpy as jnp
from jax import lax
import numpy as np

D_MODEL = 1024
BATCH = 32
SEQ = 2048
DEPTH = 2

N_EVEN = (DEPTH + 1) // 2
N_ODD = DEPTH // 2
NORM_EPS = 1e-6
ROPE_THETA = 10000.0
NEG_INF = -1e30

DN_HEADS = 4
DN_HEAD_DIM = 128
DN_WIDTH = DN_HEADS * DN_HEAD_DIM
DN_CONV = 3
DN_CHUNK = 64
DN_COLS = 4 * DN_WIDTH + 4 * DN_HEADS
DIL_HEADS = 8
DIL_HEAD_DIM = 64
DIL_WIDTH = DIL_HEADS * DIL_HEAD_DIM
DIL_PATTERNS = ((128, 1), (512, 4), (2048, 16))
AB_IN = DN_COLS + 3 * DIL_WIDTH
AB_OUT = DN_WIDTH + DIL_WIDTH
RET_HEADS = 4
RET_KEY_DIM = 64
RET_VAL_DIM = 128
RET_QK = RET_HEADS * RET_KEY_DIM
RET_V = RET_HEADS * RET_VAL_DIM
RET_CHUNK = 64
RET_COLS = 2 * RET_QK + 2 * RET_V
HY_WIDTH = 512
HY_SHORT = 3
HY_EMB = 33
HY_ORDER = 64
HY_TARGET = 1e-2
HY_FAST = 0.3
HY_SLOW = 1.5
CD_IN = RET_COLS + 3 * HY_WIDTH
CD_OUT = RET_V + HY_WIDTH
D_FF = 2816
FFN_CONV = 3

kernel_name = 'bidir_hybrid_deltanet_dilated_retention_hyena'

f32 = jnp.float32


def rmsnorm(x, w=None):
    xf = x.astype(f32)
    y = xf * lax.rsqrt(jnp.mean(xf * xf, axis=-1, keepdims=True) + NORM_EPS)
    if w is not None:
        y = y * w.astype(f32)
    return y.astype(x.dtype)


def l2norm(t):
    return t * lax.rsqrt(jnp.sum(t * t, axis=-1, keepdims=True) + 1e-6)


def dwconv(x, w):
    K, C = w.shape
    return lax.conv_general_dilated(x, w[:, None, :].astype(x.dtype), window_strides=(1,),
                                    padding=[(K // 2, K // 2)],
                                    dimension_numbers=('NWC', 'WIO', 'NWC'),
                                    feature_group_count=C)


def rope(t):
    S, E = t.shape[1], t.shape[-1]
    inv = ROPE_THETA ** (-jnp.arange(0, E, 2, dtype=f32) / E)
    ang = jnp.arange(S, dtype=f32)[:, None] * inv[None, :]
    cos, sin = jnp.cos(ang)[None, :, None, :], jnp.sin(ang)[None, :, None, :]
    tf = t.astype(f32)
    t1, t2 = tf[..., : E // 2], tf[..., E // 2:]
    return jnp.concatenate([t1 * cos - t2 * sin, t1 * sin + t2 * cos], axis=-1)


def gated_delta_chunked(q, k, v, g, beta):
    Bsz, H, S, dk = q.shape
    dv = v.shape[-1]
    C = DN_CHUNK
    N = S // C
    chunk = lambda t: t.reshape((Bsz, H, N, C) + t.shape[3:])
    q = chunk(q * dk ** -0.5)
    k = chunk(k)
    v = chunk(v)
    beta = chunk(beta)
    g = jnp.cumsum(chunk(g), axis=-1)
    k_beta = k * beta[..., None]
    lower = jnp.tril(jnp.ones((C, C), bool))
    strict = jnp.tril(jnp.ones((C, C), bool), -1)
    diff = g[..., :, None] - g[..., None, :]
    decay = jnp.where(lower, jnp.exp(jnp.where(lower, diff, 0.0)), 0.0)
    a = jnp.where(strict, jnp.einsum('bhnid,bhnjd->bhnij', k_beta, k) * decay, 0.0)
    m = a + jnp.eye(C, dtype=a.dtype)
    rhs = jnp.concatenate([v * beta[..., None], k_beta * jnp.exp(g)[..., None]], axis=-1)
    sol = lax.linalg.triangular_solve(m, rhs, left_side=True, lower=True, unit_diagonal=True)
    u, w = sol[..., :dv], sol[..., dv:]
    intra = jnp.einsum('bhnid,bhnjd->bhnij', q, k) * decay
    q_dec = q * jnp.exp(g)[..., None]
    g_last = g[..., -1]
    k_dec = k * jnp.exp(g_last[..., None] - g)[..., None]
    lead = lambda t: jnp.moveaxis(t, 2, 0)

    def step(state, inp):
        q_c, k_c, u_c, w_c, a_c, gl = inp
        v_new = u_c - jnp.einsum('bhik,bhkv->bhiv', w_c, state)
        o_c = jnp.einsum('bhik,bhkv->bhiv', q_c, state) + jnp.einsum('bhij,bhjv->bhiv', a_c, v_new)
        state = state * jnp.exp(gl)[..., None, None] + jnp.einsum('bhjk,bhjv->bhkv', k_c, v_new)
        return state, o_c

    state0 = jnp.zeros((Bsz, H, dk, dv), f32)
    _, o = lax.scan(step, state0, (lead(q_dec), lead(k_dec), lead(u), lead(w), lead(intra), lead(g_last)))
    return jnp.moveaxis(o, 0, 2).reshape(Bsz, H, S, dv)


def gated_deltanet(proj, conv_w, a_log, dt_bias, norm_w):
    Bsz, S, _ = proj.shape
    W, H, E = DN_WIDTH, DN_HEADS, DN_HEAD_DIM
    qkv = jax.nn.silu(dwconv(proj[..., : 3 * W], conv_w))
    q, k, v = [t.reshape(Bsz, S, H, E).astype(f32) for t in jnp.split(qkv, 3, axis=-1)]
    q, k = l2norm(q), l2norm(k)
    z = proj[..., 3 * W: 4 * W].reshape(Bsz, S, H, E).astype(f32)
    b = proj[..., 4 * W: 4 * W + 2 * H].reshape(Bsz, S, 2, H).astype(f32)
    a = proj[..., 4 * W + 2 * H:].reshape(Bsz, S, 2, H).astype(f32)
    beta = jax.nn.sigmoid(b)
    g = -jnp.exp(a_log.astype(f32)) * jax.nn.softplus(a + dt_bias.astype(f32))
    bhs = lambda t: jnp.swapaxes(t, 1, 2)
    rev = lambda t: jnp.flip(t, axis=2)
    q, k, v = bhs(q), bhs(k), bhs(v)
    o_f = gated_delta_chunked(q, k, v, bhs(g[:, :, 0]), bhs(beta[:, :, 0]))
    o_b = rev(gated_delta_chunked(rev(q), rev(k), rev(v), rev(bhs(g[:, :, 1])), rev(bhs(beta[:, :, 1]))))
    o = bhs(o_f + o_b)
    o = rmsnorm(o, norm_w) * jax.nn.silu(z)
    return o.reshape(Bsz, S, W)


def dilated_branch(q, k, v, window, dilation):
    Bsz, S, H, E = q.shape
    half = window // (2 * dilation)
    L = S // dilation
    blk = half
    nb = -(-L // blk)
    Lp = nb * blk
    strided = lambda t: t.reshape(Bsz, L, dilation, H, E).transpose(0, 2, 3, 1, 4)
    qs, ks, vs = strided(q), strided(k), strided(v)
    qb = jnp.pad(qs, [(0, 0)] * 3 + [(0, Lp - L), (0, 0)]).reshape(Bsz, dilation, H, nb, blk, E)

    def band(t):
        tb = jnp.pad(t, [(0, 0)] * 3 + [(blk, Lp - L + blk), (0, 0)]).reshape(Bsz, dilation, H, nb + 2, blk, E)
        return jnp.concatenate([tb[:, :, :, :-2], tb[:, :, :, 1:-1], tb[:, :, :, 2:]], axis=-2)

    kb, vb = band(ks), band(vs)
    qi = jnp.arange(nb)[:, None] * blk + jnp.arange(blk)[None, :]
    ki = jnp.arange(nb)[:, None] * blk - blk + jnp.arange(3 * blk)[None, :]
    rel = ki[:, None, :] - qi[:, :, None]
    valid = (jnp.abs(rel) <= half) & (ki[:, None, :] >= 0) & (ki[:, None, :] < L)
    s = jnp.einsum('bdhnqe,bdhnke->bdhnqk', qb, kb) * E ** -0.5
    s = jnp.where(valid, s, NEG_INF)
    mx = jnp.max(s, axis=-1, keepdims=True)
    ex = jnp.exp(s - mx)
    den = jnp.sum(ex, axis=-1, keepdims=True)
    o = jnp.einsum('bdhnqk,bdhnke->bdhnqe', ex / den, vb)
    lse = (mx + jnp.log(den))[..., 0]
    o = o.reshape(Bsz, dilation, H, Lp, E)[:, :, :, :L].transpose(0, 3, 1, 2, 4).reshape(Bsz, S, H, E)
    lse = lse.reshape(Bsz, dilation, H, Lp)[..., :L].transpose(0, 3, 1, 2).reshape(Bsz, S, H)
    return o, lse


def dilated_attention(proj):
    Bsz, S, _ = proj.shape
    shp = (Bsz, S, DIL_HEADS, DIL_HEAD_DIM)
    q, k, v = jnp.split(proj, 3, axis=-1)
    q, k = rope(q.reshape(shp)), rope(k.reshape(shp))
    v = v.reshape(shp).astype(f32)
    outs, lses = [], []
    for window, dilation in DIL_PATTERNS:
        o, l = dilated_branch(q, k, v, window, dilation)
        outs.append(o)
        lses.append(l)
    wts = jax.nn.softmax(jnp.stack(lses), axis=0)
    o = jnp.einsum('pbsh,pbshe->bshe', wts, jnp.stack(outs))
    return o.reshape(Bsz, S, DIL_WIDTH)


def mixer_ab(h, w_in, conv_w, a_log, dt_bias, norm_w, w_out):
    proj = h @ w_in
    y_a = gated_deltanet(proj[..., :DN_COLS], conv_w, a_log, dt_bias, norm_w)
    y_b = dilated_attention(proj[..., DN_COLS:])
    return (jnp.concatenate([y_a, y_b], axis=-1) @ w_out).astype(h.dtype)


def retention_chunked(q, k, v, log_gamma, include_diag):
    Bsz, H, S, dk = q.shape
    dv = v.shape[-1]
    C = RET_CHUNK
    N = S // C
    chunk = lambda t: t.reshape(Bsz, H, N, C, t.shape[-1])
    q, k, v = chunk(q), chunk(k), chunk(v)
    log_gamma = log_gamma.astype(f32)
    idx = jnp.arange(C, dtype=f32)
    rel = idx[:, None] - idx[None, :]
    mask = rel >= 0 if include_diag else rel > 0
    dmat = jnp.where(mask, jnp.exp(log_gamma[:, None, None] * jnp.where(mask, rel, 0.0)), 0.0)
    scores = jnp.einsum('bhnid,bhnjd->bhnij', q, k) * dmat[:, None]
    o = jnp.einsum('bhnij,bhnje->bhnie', scores, v)
    lgv = log_gamma[:, None]
    k_dec = k * jnp.exp(lgv * (C - 1 - idx))[:, None, :, None]
    kv = jnp.einsum('bhnjd,bhnje->nbhde', k_dec, v)
    chunk_decay = jnp.exp(log_gamma * C)[:, None, None]

    def step(state, kv_c):
        return state * chunk_decay + kv_c, state

    _, prev = lax.scan(step, jnp.zeros((Bsz, H, dk, dv), f32), kv)
    q_dec = q * jnp.exp(lgv * (idx + 1))[:, None, :, None]
    o = o + jnp.einsum('bhnid,nbhde->bhnie', q_dec, prev)
    return o.reshape(Bsz, H, S, dv)


def retention(proj, log_decay):
    Bsz, S, _ = proj.shape
    H = RET_HEADS
    q = proj[..., :RET_QK].reshape(Bsz, S, H, RET_KEY_DIM)
    k = proj[..., RET_QK: 2 * RET_QK].reshape(Bsz, S, H, RET_KEY_DIM)
    v = proj[..., 2 * RET_QK: 2 * RET_QK + RET_V].reshape(Bsz, S, H, RET_VAL_DIM).astype(f32)
    gate = proj[..., 2 * RET_QK + RET_V:].reshape(Bsz, S, H, RET_VAL_DIM).astype(f32)
    q = rope(q) * RET_KEY_DIM ** -0.5
    k = rope(k)
    bhs = lambda t: jnp.swapaxes(t, 1, 2)
    rev = lambda t: jnp.flip(t, axis=2)
    q, k, v = bhs(q), bhs(k), bhs(v)
    o_f = retention_chunked(q, k, v, log_decay[0], True)
    o_b = rev(retention_chunked(rev(q), rev(k), rev(v), log_decay[1], False))
    o = rmsnorm(bhs(o_f + o_b)) * jax.nn.silu(gate)
    return o.reshape(Bsz, S, RET_V)


def hyena_filters(L, w1, b1, f1, w2, b2, f2, w3):
    t = jnp.linspace(0.0, 1.0, L, dtype=f32)[:, None]
    bands = (HY_EMB - 1) // 2
    w = 2.0 * math.pi * jnp.arange(L, dtype=f32) / L
    f = jnp.linspace(1e-4, bands - 1, bands, dtype=f32)
    ang = w[:, None] * f[None, :]
    z = jnp.concatenate([t, jnp.cos(ang), -jnp.sin(ang)], axis=-1)
    hid = jnp.sin(f1 * (z @ w1 + b1))
    hid = jnp.sin(f2 * (hid @ w2 + b2))
    filt = (hid @ w3).astype(f32).reshape(L, 2, HY_WIDTH)
    deltas = jnp.abs(jnp.linspace(math.log(HY_TARGET) / HY_SLOW, math.log(HY_TARGET) / HY_FAST, HY_WIDTH, dtype=f32))
    filt = filt * jnp.exp(-t * deltas)[:, None, :]
    return filt[:, 0], filt[:, 1]


def long_conv_bidir(u, h_fwd, h_bwd):
    L = u.shape[1]
    n = 2 * L
    spec = jnp.fft.rfft(h_fwd, n=n, axis=0) + jnp.conj(jnp.fft.rfft(h_bwd, n=n, axis=0))
    y = jnp.fft.irfft(jnp.fft.rfft(u, n=n, axis=1) * spec[None], n=n, axis=1)
    return y[:, :L]


def hyena(proj, conv_w, conv_b, w1, b1, f1, w2, b2, f2, w3, bias):
    S = proj.shape[1]
    uc = (dwconv(proj, conv_w) + conv_b).astype(f32)
    x0, x1, v = jnp.split(uc, 3, axis=-1)
    h_fwd, h_bwd = hyena_filters(S, w1, b1, f1, w2, b2, f2, w3)
    v = v * x1
    v = long_conv_bidir(v, h_fwd, h_bwd) + v * bias.astype(f32)
    return v * x0


def mixer_cd(h, w_in, log_decay, conv_w, conv_b, w1, b1, f1, w2, b2, f2, w3, bias, w_out):
    proj = h @ w_in
    y_c = retention(proj[..., :RET_COLS], log_decay)
    y_d = hyena(proj[..., RET_COLS:], conv_w, conv_b, w1, b1, f1, w2, b2, f2, w3, bias)
    return (jnp.concatenate([y_c, y_d], axis=-1) @ w_out).astype(h.dtype)


def conv_ffn(h, w_in, conv_w, conv_b, w_out):
    gate, up = jnp.split(h @ w_in, 2, axis=-1)
    gate = dwconv(gate, conv_w) + conv_b
    return (jax.nn.silu(gate) * up) @ w_out


def setup_inputs(seed: int = 0) -> dict:
    key = jax.random.key(seed)
    ks = iter(jax.random.split(key, 32))
    nrm = lambda shape, std: jax.random.normal(next(ks), shape, f32) * std
    dense = lambda shape: nrm(shape, shape[-2] ** -0.5)
    x = nrm((BATCH, SEQ, D_MODEL), 1.0)
    norm_mix = 1.0 + nrm((DEPTH, D_MODEL), 0.02)
    norm_ffn = 1.0 + nrm((DEPTH, D_MODEL), 0.02)
    final_norm = 1.0 + nrm((D_MODEL,), 0.02)
    ab_w_in = dense((N_EVEN, D_MODEL, AB_IN))
    dn_conv_w = nrm((N_EVEN, DN_CONV, 3 * DN_WIDTH), DN_CONV ** -0.5)
    dn_a_log = jnp.log(jax.random.uniform(next(ks), (N_EVEN, 2, DN_HEADS), f32, 1.0, 16.0))
    dt = jnp.exp(jax.random.uniform(next(ks), (N_EVEN, 2, DN_HEADS), f32, math.log(1e-3), math.log(1e-1)))
    dn_dt_bias = dt + jnp.log(-jnp.expm1(-dt))
    dn_norm_w = 1.0 + nrm((N_EVEN, DN_HEAD_DIM), 0.02)
    ab_w_out = dense((N_EVEN, AB_OUT, D_MODEL))
    cd_w_in = dense((N_ODD, D_MODEL, CD_IN))
    base = jnp.log1p(-(2.0 ** (-5.0 - jnp.arange(RET_HEADS, dtype=f32))))
    ret_log_decay = base * (1.0 + nrm((N_ODD, 2, RET_HEADS), 0.01))
    hy_conv_w = nrm((N_ODD, HY_SHORT, 3 * HY_WIDTH), HY_SHORT ** -0.5)
    hy_conv_b = nrm((N_ODD, 3 * HY_WIDTH), 0.02)
    hy_w1 = dense((N_ODD, HY_EMB, HY_ORDER))
    hy_b1 = nrm((N_ODD, HY_ORDER), 0.1)
    hy_f1 = 1.0 + nrm((N_ODD, HY_ORDER), 0.01)
    hy_w2 = dense((N_ODD, HY_ORDER, HY_ORDER))
    hy_b2 = nrm((N_ODD, HY_ORDER), 0.1)
    hy_f2 = 1.0 + nrm((N_ODD, HY_ORDER), 0.01)
    hy_w3 = nrm((N_ODD, HY_ORDER, 2 * HY_WIDTH), 0.05 * HY_ORDER ** -0.5)
    hy_bias = nrm((N_ODD, HY_WIDTH), 1.0)
    cd_w_out = dense((N_ODD, CD_OUT, D_MODEL))
    ffn_w_in = dense((DEPTH, D_MODEL, 2 * D_FF))
    ffn_conv_w = nrm((DEPTH, FFN_CONV, D_FF), FFN_CONV ** -0.5)
    ffn_conv_b = nrm((DEPTH, D_FF), 0.02)
    ffn_w_out = dense((DEPTH, D_FF, D_MODEL))
    return {'x': x, 'norm_mix': norm_mix, 'norm_ffn': norm_ffn, 'final_norm': final_norm,
            'ab_w_in': ab_w_in, 'dn_conv_w': dn_conv_w, 'dn_a_log': dn_a_log, 'dn_dt_bias': dn_dt_bias,
            'dn_norm_w': dn_norm_w, 'ab_w_out': ab_w_out,
            'cd_w_in': cd_w_in, 'ret_log_decay': ret_log_decay, 'hy_conv_w': hy_conv_w, 'hy_conv_b': hy_conv_b,
            'hy_w1': hy_w1, 'hy_b1': hy_b1, 'hy_f1': hy_f1, 'hy_w2': hy_w2, 'hy_b2': hy_b2, 'hy_f2': hy_f2,
            'hy_w3': hy_w3, 'hy_bias': hy_bias, 'cd_w_out': cd_w_out,
            'ffn_w_in': ffn_w_in, 'ffn_conv_w': ffn_conv_w, 'ffn_conv_b': ffn_conv_b, 'ffn_w_out': ffn_w_out}


def reference(x, norm_mix, norm_ffn, final_norm,
              ab_w_in, dn_conv_w, dn_a_log, dn_dt_bias, dn_norm_w, ab_w_out,
              cd_w_in, ret_log_decay, hy_conv_w, hy_conv_b, hy_w1, hy_b1, hy_f1,
              hy_w2, hy_b2, hy_f2, hy_w3, hy_bias, cd_w_out,
              ffn_w_in, ffn_conv_w, ffn_conv_b, ffn_w_out):
    for layer in range(DEPTH):
        i = layer // 2
        h = rmsnorm(x, norm_mix[layer])
        if layer % 2 == 0:
            mixed = mixer_ab(h, ab_w_in[i], dn_conv_w[i], dn_a_log[i], dn_dt_bias[i], dn_norm_w[i], ab_w_out[i])
        else:
            mixed = mixer_cd(h, cd_w_in[i], ret_log_decay[i], hy_conv_w[i], hy_conv_b[i],
                             hy_w1[i], hy_b1[i], hy_f1[i], hy_w2[i], hy_b2[i], hy_f2[i], hy_w3[i],
                             hy_bias[i], cd_w_out[i])
        x = x + mixed
        x = x + conv_ffn(rmsnorm(x, norm_ffn[layer]), ffn_w_in[layer], ffn_conv_w[layer],
                         ffn_conv_b[layer], ffn_w_out[layer])
    return rmsnorm(x, final_norm)
```

```python
import functools
import math

import numpy as np
import jax
import jax.numpy as jnp
from jax import lax
from jax.experimental import pallas as pl
from jax.experimental.pallas import tpu as pltpu

F32 = jnp.float32
BF16 = jnp.bfloat16
HIGHEST = lax.Precision.HIGHEST

NORM_EPS = 1e-6
ROPE_THETA = 10000.0
NEG_INF = -1e30

D_MODEL = 1024
DN_HEADS = 4
DN_HEAD_DIM = 128
DN_WIDTH = DN_HEADS * DN_HEAD_DIM
DN_CHUNK = 64
DN_CHAINS = 2 * DN_HEADS
DIL_HEADS = 8
DIL_HEAD_DIM = 64
DIL_WIDTH = DIL_HEADS * DIL_HEAD_DIM
DIL_PATTERNS = ((128, 1), (512, 4), (2048, 16))
DIL_QBLK = 128
RET_HEADS = 4
RET_KEY_DIM = 64
RET_VAL_DIM = 128
RET_QK = RET_HEADS * RET_KEY_DIM
RET_V = RET_HEADS * RET_VAL_DIM
RET_CHUNK = 256
HY_WIDTH = 512
HY_EMB = 33
HY_ORDER = 64
HY_TARGET = 1e-2
HY_FAST = 0.3
HY_SLOW = 1.5
HY_FREQ_TILE = 256
D_FF = 2816
FF_TILE = 256

LANES = 128
VMEM_LIMIT = 56 * 1024 * 1024


def _cparams(*sem):
    return pltpu.CompilerParams(dimension_semantics=sem, vmem_limit_bytes=VMEM_LIMIT)


def _dot(a, b):
    return jnp.dot(a, b, preferred_element_type=F32)


def _dot_nt(a, b):
    return lax.dot_general(a, b, (((1,), (1,)), ((), ())), preferred_element_type=F32)


def _dot_tn(a, b):
    return lax.dot_general(a, b, (((0,), (0,)), ((), ())), preferred_element_type=F32)


def _silu(x):
    return x * (1.0 / (1.0 + jnp.exp(-x)))


def _shift_rows(x, row, n_rows):
    prev = jnp.where(row == 0, 0.0, pltpu.roll(x, 1, axis=0))
    nxt = jnp.where(row == n_rows - 1, 0.0, pltpu.roll(x, n_rows - 1, axis=0))
    return prev, nxt


def _proj_kernel(x_ref, nw_ref, *refs):
    n = len(refs) // 2
    x = x_ref[...]
    h = x * lax.rsqrt(jnp.mean(x * x, axis=-1, keepdims=True) + NORM_EPS) * nw_ref[...]
    hb = h.astype(BF16)
    for w_ref, o_ref in zip(refs[:n], refs[n:]):
        o_ref[...] = _dot(hb, w_ref[...]).astype(o_ref.dtype)


def _norm_proj(x2d, norm_w, weights, out_dtypes, tm=512):
    T, D = x2d.shape
    in_specs = [pl.BlockSpec((tm, D), lambda i: (i, 0)),
                pl.BlockSpec((1, D), lambda i: (0, 0))]
    in_specs += [pl.BlockSpec(w.shape, lambda i: (0, 0)) for w in weights]
    out_specs = [pl.BlockSpec((tm, w.shape[1]), lambda i: (i, 0)) for w in weights]
    out_shape = [jax.ShapeDtypeStruct((T, w.shape[1]), dt) for w, dt in zip(weights, out_dtypes)]
    return pl.pallas_call(
        _proj_kernel, grid=(T // tm,), in_specs=in_specs, out_specs=out_specs,
        out_shape=out_shape, compiler_params=_cparams("parallel"), name="norm_proj",
    )(x2d, norm_w.reshape(1, D), *weights)


def _outproj_kernel(x_ref, ya_ref, yb_ref, wa_ref, wb_ref, o_ref):
    o_ref[...] = x_ref[...] + _dot(ya_ref[...], wa_ref[...]) + _dot(yb_ref[...], wb_ref[...])


def _out_proj(x2d, ya, yb, wa, wb, tm=1024):
    T, D = x2d.shape
    row = lambda i: (i, 0)
    full = lambda i: (0, 0)
    return pl.pallas_call(
        _outproj_kernel, grid=(T // tm,),
        in_specs=[pl.BlockSpec((tm, D), row), pl.BlockSpec((tm, ya.shape[1]), row),
                  pl.BlockSpec((tm, yb.shape[1]), row), pl.BlockSpec(wa.shape, full),
                  pl.BlockSpec(wb.shape, full)],
        out_specs=pl.BlockSpec((tm, D), row),
        out_shape=jax.ShapeDtypeStruct((T, D), F32),
        compiler_params=_cparams("parallel"), name="out_proj",
    )(x2d, ya, yb, wa, wb)


def _ffn_kernel(x_ref, nw_ref, wg_ref, wu_ref, cw_ref, cb_ref, wo_ref, fw_ref, o_ref, h_ref,
                *, final_norm):
    j = pl.program_id(1)
    S = x_ref.shape[1]

    @pl.when(j == 0)
    def _():
        x = x_ref[0]
        h = x * lax.rsqrt(jnp.mean(x * x, axis=-1, keepdims=True) + NORM_EPS) * nw_ref[...]
        h_ref[...] = h.astype(BF16)
        o_ref[0] = x

    hb = h_ref[...]
    gate = _dot(hb, wg_ref[...])
    up = _dot(hb, wu_ref[...])
    row = lax.broadcasted_iota(jnp.int32, gate.shape, 0)
    prev, nxt = _shift_rows(gate, row, S)
    cw = cw_ref[...]
    g = prev * cw[0:1] + gate * cw[1:2] + nxt * cw[2:3] + cb_ref[...]
    act = (_silu(g) * up).astype(BF16)
    o_ref[0] += _dot(act, wo_ref[...])

    if final_norm:
        @pl.when(j == pl.num_programs(1) - 1)
        def _():
            y = o_ref[0]
            o_ref[0] = y * lax.rsqrt(jnp.mean(y * y, axis=-1, keepdims=True) + NORM_EPS) * fw_ref[...]


def _conv_ffn(x, norm_w, w_in, conv_w, conv_b, w_out, final_w, final_norm):
    B, S, D = x.shape
    tf = FF_TILE
    nj = D_FF // tf
    wg = w_in[:, :D_FF].astype(BF16)
    wu = w_in[:, D_FF:].astype(BF16)
    wo = w_out.astype(BF16)
    return pl.pallas_call(
        functools.partial(_ffn_kernel, final_norm=final_norm), grid=(B, nj),
        in_specs=[pl.BlockSpec((1, S, D), lambda b, j: (b, 0, 0)),
                  pl.BlockSpec((1, D), lambda b, j: (0, 0)),
                  pl.BlockSpec((D, tf), lambda b, j: (0, j)),
                  pl.BlockSpec((D, tf), lambda b, j: (0, j)),
                  pl.BlockSpec((3, tf), lambda b, j: (0, j)),
                  pl.BlockSpec((1, tf), lambda b, j: (0, j)),
                  pl.BlockSpec((tf, D), lambda b, j: (j, 0)),
                  pl.BlockSpec((1, D), lambda b, j: (0, 0))],
        out_specs=pl.BlockSpec((1, S, D), lambda b, j: (b, 0, 0)),
        out_shape=jax.ShapeDtypeStruct((B, S, D), F32),
        scratch_shapes=[pltpu.VMEM((S, D), BF16)],
        compiler_params=_cparams("parallel", "arbitrary"), name="conv_ffn",
    )(x, norm_w.reshape(1, D), wg, wu, conv_w, conv_b.reshape(1, D_FF), wo, final_w.reshape(1, D))


def _deltanet_kernel(qkv_ref, z_ref, ba_ref, cw_ref, gp_ref, nw_ref, o_ref,
                     q_s, k_s, v_s, pk_s, of_s, ob_s, st_s):
    S = qkv_ref.shape[1]
    W, E, C, H = DN_WIDTH, DN_HEAD_DIM, DN_CHUNK, DN_HEADS
    n_chunks = S // C
    row = lax.broadcasted_iota(jnp.int32, (S, LANES), 0)
    lane = lax.broadcasted_iota(jnp.int32, (S, LANES), 1)

    cw = cw_ref[...]
    for t in range(3 * W // LANES):
        sl = slice(t * LANES, (t + 1) * LANES)
        x = qkv_ref[0, :, sl].astype(F32)
        prev, nxt = _shift_rows(x, row, S)
        y = _silu(prev * cw[0:1, sl] + x * cw[1:2, sl] + nxt * cw[2:3, sl])
        h = t % H
        hs = slice(h * E, (h + 1) * E)
        if t < H:
            y = y * lax.rsqrt(jnp.sum(y * y, axis=-1, keepdims=True) + 1e-6) * (E ** -0.5)
            q_s[:, hs] = y.astype(BF16)
        elif t < 2 * H:
            y = y * lax.rsqrt(jnp.sum(y * y, axis=-1, keepdims=True) + 1e-6)
            k_s[:, hs] = y.astype(BF16)
        else:
            v_s[:, hs] = y.astype(BF16)

    ba = ba_ref[0]
    beta = 1.0 / (1.0 + jnp.exp(-ba))
    xg = ba + gp_ref[1:2, :]
    g = gp_ref[0:1, :] * (jnp.maximum(xg, 0.0) + jnp.log1p(jnp.exp(-jnp.abs(xg))))
    pos = row & (C - 1)
    pre = g
    suf = g
    sh = 1
    while sh < C:
        pre = pre + jnp.where(pos >= sh, pltpu.roll(pre, sh, axis=0), 0.0)
        suf = suf + jnp.where(pos < C - sh, pltpu.roll(suf, S - sh, axis=0), 0.0)
        sh *= 2
    fwd = lane < 8 + H
    gc = jnp.where(fwd, pre, suf)
    glc = jnp.where(fwd, suf, pre) - g
    pk = jnp.where(lane < 8, beta,
                   jnp.where(lane < 16, gc,
                             jnp.where(lane < 24, pltpu.roll(glc, 8, axis=1),
                                       pltpu.roll(gc + glc, 16, axis=1))))
    pk_s[...] = pk

    st_s[...] = jnp.zeros_like(st_s)

    ri = lax.broadcasted_iota(jnp.int32, (C, C), 0)
    ci = lax.broadcasted_iota(jnp.int32, (C, C), 1)
    eye = (ri == ci).astype(F32)

    def chunk_body(n, carry):
        for d in range(2):
            nd = n if d == 0 else n_chunks - 1 - n
            rows = pl.ds(pl.multiple_of(nd * C, C), C)
            pkc = pk_s[rows, :]
            ex = jnp.exp(pkc)
            tr = jnp.transpose(pkc)
            causal = (ri >= ci) if d == 0 else (ri <= ci)
            strict = (ri > ci) if d == 0 else (ri < ci)
            out_s = of_s if d == 0 else ob_s
            for h in range(H):
                c = d * H + h
                hs = slice(h * E, (h + 1) * E)
                q_c = q_s[rows, hs]
                k_c = k_s[rows, hs]
                v_c = v_s[rows, hs]
                beta_col = pkc[:, c:c + 1]
                gc_col = pkc[:, 8 + c:9 + c]
                eg_col = ex[:, 8 + c:9 + c]
                eglc_col = ex[:, 16 + c:17 + c]
                etot_col = ex[:, 24 + c:25 + c]
                beta_row = tr[c:c + 1, :]
                gc_row = tr[8 + c:9 + c, :]
                beg_row = beta_row * jnp.exp(gc_row)
                dmat = jnp.where(causal, jnp.exp(jnp.where(causal, gc_col - gc_row, 0.0)), 0.0)
                s2 = _dot_nt(jnp.concatenate([q_c, k_c], axis=0), k_c)
                intra = s2[:C] * dmat
                p = -jnp.where(strict, s2[C:] * beta_col * dmat, 0.0)
                t_inv = eye + p
                pw = _dot(p.astype(BF16), p.astype(BF16))
                for it in range(1, 6):
                    pwb = pw.astype(BF16)
                    if it < 5:
                        zz = _dot(pwb, jnp.concatenate([pwb, t_inv.astype(BF16)], axis=1))
                        pw = zz[:, :C]
                        t_inv = t_inv + zz[:, C:]
                    else:
                        t_inv = t_inv + _dot(pwb, t_inv.astype(BF16))
                u = _dot((t_inv * beta_row).astype(BF16), v_c)
                w = _dot((t_inv * beg_row).astype(BF16), k_c)
                st = st_s[c]
                wq = _dot(jnp.concatenate([w.astype(BF16), q_c], axis=0), st.astype(BF16))
                v_new = u - wq[:C]
                o = eg_col * wq[C:] + _dot(intra.astype(BF16), v_new.astype(BF16))
                upd = _dot_tn(k_c, (eglc_col * v_new).astype(BF16))
                st_s[c] = jnp.concatenate([etot_col, etot_col], axis=0) * st + upd
                out_s[rows, hs] = o
        return carry

    lax.fori_loop(0, n_chunks, chunk_body, 0)

    nw = nw_ref[...]
    for h in range(H):
        hs = slice(h * E, (h + 1) * E)
        o = of_s[:, hs] + ob_s[:, hs]
        o = o * lax.rsqrt(jnp.mean(o * o, axis=-1, keepdims=True) + NORM_EPS) * nw
        o_ref[0, :, hs] = (o * _silu(z_ref[0, :, hs].astype(F32))).astype(o_ref.dtype)


def _deltanet(qkv, z, ba, conv_w, a_log, dt_bias, norm_w):
    B, S, _ = qkv.shape
    W, E = DN_WIDTH, DN_HEAD_DIM
    gp = jnp.zeros((2, LANES), F32)
    gp = gp.at[0, 8:16].set(-jnp.exp(a_log.astype(F32)).reshape(-1))
    gp = gp.at[1, 8:16].set(dt_bias.astype(F32).reshape(-1))
    bspec = lambda n: pl.BlockSpec((1, S, n), lambda b: (b, 0, 0))
    full = lambda shp: pl.BlockSpec(shp, lambda b: (0,) * len(shp))
    return pl.pallas_call(
        _deltanet_kernel, grid=(B,),
        in_specs=[bspec(3 * W), bspec(W), bspec(LANES), full((3, 3 * W)), full((2, LANES)),
                  full((1, E))],
        out_specs=bspec(W),
        out_shape=jax.ShapeDtypeStruct((B, S, W), BF16),
        scratch_shapes=[pltpu.VMEM((S, W), BF16), pltpu.VMEM((S, W), BF16), pltpu.VMEM((S, W), BF16),
                        pltpu.VMEM((S, LANES), F32), pltpu.VMEM((S, W), F32), pltpu.VMEM((S, W), F32),
                        pltpu.VMEM((DN_CHAINS, E, E), F32)],
        compiler_params=_cparams("parallel"), name="deltanet",
    )(qkv, z, ba, conv_w.astype(F32), gp, norm_w.reshape(1, E).astype(F32))


def _rope_tables(S, E):
    inv = ROPE_THETA ** (-jnp.arange(0, E, 2, dtype=F32) / E)
    ang = jnp.arange(S, dtype=F32)[:, None] * inv[None, :]
    cos, sin = jnp.cos(ang), jnp.sin(ang)
    cos_t = jnp.tile(jnp.concatenate([cos, cos], axis=-1), (1, LANES // E))
    sin_t = jnp.tile(jnp.concatenate([-sin, sin], axis=-1), (1, LANES // E))
    return cos_t, sin_t


def _rope_tile(x, cos_t, sin_t, lane):
    half = DIL_HEAD_DIM // 2
    partner = jnp.where((lane & (DIL_HEAD_DIM - 1)) < half,
                        pltpu.roll(x, LANES - half, axis=1), pltpu.roll(x, half, axis=1))
    return x * cos_t + partner * sin_t


def _dilated_kernel(q_ref, k_ref, v_ref, cos_ref, sin_ref, o_ref,
                    q_s, k_s, v_s, op_s, lse_s):
    S = q_ref.shape[1]
    E = DIL_HEAD_DIM
    QB = DIL_QBLK
    lane = lax.broadcasted_iota(jnp.int32, (S, LANES), 1)
    cos_t = cos_ref[...]
    sin_t = sin_ref[...]
    q_s[...] = _rope_tile(q_ref[0].astype(F32), cos_t, sin_t, lane) * (E ** -0.5)
    k_s[...] = _rope_tile(k_ref[0].astype(F32), cos_t, sin_t, lane)
    v_s[...] = v_ref[0].astype(F32)

    for p, (window, dil) in enumerate(DIL_PATTERNS):
        half = window // (2 * dil)
        L = S // dil
        nb = L // QB
        KW = min(QB + 2 * half, L)
        relm = (lax.broadcasted_iota(jnp.int32, (QB, KW), 1)
                - lax.broadcasted_iota(jnp.int32, (QB, KW), 0))

        def block(it, carry, dil=dil, half=half, L=L, nb=nb, KW=KW, relm=relm, p=p):
            r = it // nb
            n = it % nb
            q0 = n * QB
            k0 = jnp.clip(q0 - half, 0, L - KW)
            if dil == 1:
                qrows = pl.ds(pl.multiple_of(q0, QB), QB)
                krows = pl.ds(pl.multiple_of(k0, half), KW)
            else:
                qrows = pl.ds(r + dil * q0, QB, stride=dil)
                krows = pl.ds(r + dil * k0, KW, stride=dil)
            valid = jnp.abs(relm + (k0 - q0)) <= half
            qb = q_s[qrows, :].astype(BF16)
            kb = k_s[krows, :].astype(BF16)
            vb = v_s[krows, :].astype(BF16)
            outs, lses = [], []
            for hh in range(LANES // E):
                hs = slice(hh * E, (hh + 1) * E)
                s = jnp.where(valid, _dot_nt(qb[:, hs], kb[:, hs]), NEG_INF)
                m = jnp.max(s, axis=-1, keepdims=True)
                e = jnp.exp(s - m)
                den = jnp.sum(e, axis=-1, keepdims=True)
                o = _dot(e.astype(BF16), vb[:, hs]) * (1.0 / den)
                outs.append(o)
                lses.append(jnp.broadcast_to(m + jnp.log(den), (QB, E)))
            op_s[p, qrows, :] = jnp.concatenate(outs, axis=1)
            lse_s[p, qrows, :] = jnp.concatenate(lses, axis=1)
            return carry

        lax.fori_loop(0, dil * nb, block, 0)

    l0, l1, l2 = lse_s[0], lse_s[1], lse_s[2]
    mx = jnp.maximum(jnp.maximum(l0, l1), l2)
    w0, w1, w2 = jnp.exp(l0 - mx), jnp.exp(l1 - mx), jnp.exp(l2 - mx)
    out = (w0 * op_s[0] + w1 * op_s[1] + w2 * op_s[2]) * (1.0 / (w0 + w1 + w2))
    o_ref[0] = out.astype(o_ref.dtype)


def _dilated_attention(qkv, cos_t, sin_t):
    B, S, _ = qkv.shape
    nt = DIL_WIDTH // LANES
    tile = lambda off: pl.BlockSpec((1, S, LANES), lambda b, t: (b, 0, t + off))
    tab = pl.BlockSpec((S, LANES), lambda b, t: (0, 0))
    return pl.pallas_call(
        _dilated_kernel, grid=(B, nt),
        in_specs=[tile(0), tile(nt), tile(2 * nt), tab, tab],
        out_specs=tile(0),
        out_shape=jax.ShapeDtypeStruct((B, S, DIL_WIDTH), BF16),
        scratch_shapes=[pltpu.VMEM((S, LANES), F32)] * 3
        + [pltpu.VMEM((len(DIL_PATTERNS), S, LANES), F32)] * 2,
        compiler_params=_cparams("parallel", "parallel"), name="dilated_attention",
    )(qkv, qkv, qkv, cos_t, sin_t)


def _retention_kernel(lg_ref, qk_ref, v_ref, g_ref, cos_ref, sin_ref, o_ref,
                      qk_s, d_s, kv_s, o_s):
    S = qk_ref.shape[1]
    H, EK, EV, C = RET_HEADS, RET_KEY_DIM, RET_VAL_DIM, RET_CHUNK
    n_chunks = S // C
    lane = lax.broadcasted_iota(jnp.int32, (S, LANES), 1)
    cos_t = cos_ref[...]
    sin_t = sin_ref[...]
    for t in range(2 * RET_QK // LANES):
        sl = slice(t * LANES, (t + 1) * LANES)
        y = _rope_tile(qk_ref[0, :, sl].astype(F32), cos_t, sin_t, lane)
        qk_s[:, sl] = y * (EK ** -0.5) if t < RET_QK // LANES else y

    rel = (lax.broadcasted_iota(jnp.int32, (C, C), 0)
           - lax.broadcasted_iota(jnp.int32, (C, C), 1)).astype(F32)
    idx = lax.broadcasted_iota(jnp.int32, (C, EK), 0).astype(F32)
    for h in range(H):
        lgf = lg_ref[0, h]
        lgb = lg_ref[1, h]
        d_s[h] = jnp.exp(lgf * jnp.maximum(rel, 0.0) + lgb * jnp.maximum(-rel, 0.0))
        k_dec = jnp.concatenate([jnp.exp(lgf * (C - 1 - idx)), jnp.exp(lgb * idx)], axis=1)
        q_dec = jnp.concatenate([jnp.exp(lgf * (idx + 1)), jnp.exp(lgb * (C - idx))], axis=1)
        qs = slice(h * EK, (h + 1) * EK)
        ks = slice(RET_QK + h * EK, RET_QK + (h + 1) * EK)
        vs = slice(h * EV, (h + 1) * EV)
        for n in range(n_chunks):
            rows = slice(n * C, (n + 1) * C)
            k_c = qk_s[rows, ks]
            k2 = (jnp.concatenate([k_c, k_c], axis=1) * k_dec).astype(BF16)
            kv_s[n] = _dot_tn(k2, v_ref[0, rows, vs])
        fdec = jnp.exp(lgf * C)
        bdec = jnp.exp(lgb * C)
        st = jnp.zeros((EK, EV), F32)
        for n in range(n_chunks):
            inc = kv_s[n, :EK, :]
            kv_s[n, :EK, :] = st
            st = st * fdec + inc
        st = jnp.zeros((EK, EV), F32)
        for n in range(n_chunks - 1, -1, -1):
            inc = kv_s[n, EK:, :]
            kv_s[n, EK:, :] = st
            st = st * bdec + inc
        for n in range(n_chunks):
            rows = slice(n * C, (n + 1) * C)
            q_c = qk_s[rows, qs]
            k_c = qk_s[rows, ks]
            sc = (_dot_nt(q_c.astype(BF16), k_c.astype(BF16)) * d_s[h]).astype(BF16)
            q2 = (jnp.concatenate([q_c, q_c], axis=1) * q_dec).astype(BF16)
            o_s[rows, vs] = _dot(sc, v_ref[0, rows, vs]) + _dot(q2, kv_s[n].astype(BF16))

    for h in range(H):
        vs = slice(h * EV, (h + 1) * EV)
        o = o_s[:, vs]
        o = o * lax.rsqrt(jnp.mean(o * o, axis=-1, keepdims=True) + NORM_EPS)
        o_ref[0, :, vs] = (o * _silu(g_ref[0, :, vs].astype(F32))).astype(o_ref.dtype)


def _retention(qk, v, gate, log_decay, cos_t, sin_t):
    B, S, _ = qk.shape
    H, EK, EV, C = RET_HEADS, RET_KEY_DIM, RET_VAL_DIM, RET_CHUNK
    bspec = lambda n: pl.BlockSpec((1, S, n), lambda b: (b, 0, 0))
    tab = pl.BlockSpec((S, LANES), lambda b: (0, 0))
    return pl.pallas_call(
        _retention_kernel, grid=(B,),
        in_specs=[pl.BlockSpec(memory_space=pltpu.SMEM), bspec(2 * RET_QK), bspec(RET_V),
                  bspec(RET_V), tab, tab],
        out_specs=bspec(RET_V),
        out_shape=jax.ShapeDtypeStruct((B, S, RET_V), BF16),
        scratch_shapes=[pltpu.VMEM((S, 2 * RET_QK), F32), pltpu.VMEM((H, C, C), F32),
                        pltpu.VMEM((S // C, 2 * EK, EV), F32), pltpu.VMEM((S, RET_V), F32)],
        compiler_params=_cparams("parallel"), name="retention",
    )(log_decay.astype(F32), qk, v, gate, cos_t, sin_t)


def _dft_matrix(S):
    n = 2 * S
    k = jnp.arange(S, dtype=jnp.int32)[:, None]
    t = jnp.arange(S, dtype=jnp.int32)[None, :]
    ang = ((k * t) % n).astype(F32) * (2.0 * math.pi / n)
    cosb = jnp.cos(ang)
    sinb = -jnp.sin(ang)
    nyq = jnp.where(t % 2 == 0, 1.0, -1.0).astype(F32)
    sinb = jnp.where(k == 0, nyq, sinb)
    return jnp.stack([cosb, sinb])


def _hy_filter_kernel(z_ref, w1_ref, b1_ref, f1_ref, w2_ref, b2_ref, f2_ref, w3_ref, dl_ref, o_ref):
    dot = functools.partial(jnp.dot, preferred_element_type=F32, precision=HIGHEST)
    z = z_ref[...]
    hid = jnp.sin(f1_ref[...] * (dot(z, w1_ref[...]) + b1_ref[...]))
    hid = jnp.sin(f2_ref[...] * (dot(hid, w2_ref[...]) + b2_ref[...]))
    o_ref[...] = dot(hid, w3_ref[...]) * jnp.exp(-z[:, 0:1] * dl_ref[...])


def _hy_spectrum_kernel(f_ref, h_ref, o_ref):
    o_ref[0] = jnp.dot(f_ref[0], h_ref[...], preferred_element_type=F32, precision=HIGHEST)


def _hyena_spectrum(S, dft, w1, b1, f1, w2, b2, f2, w3):
    t = jnp.linspace(0.0, 1.0, S, dtype=F32)[:, None]
    bands = (HY_EMB - 1) // 2
    wv = 2.0 * math.pi * jnp.arange(S, dtype=F32) / S
    fr = jnp.linspace(1e-4, bands - 1, bands, dtype=F32)
    ang = wv[:, None] * fr[None, :]
    z = jnp.concatenate([t, jnp.cos(ang), -jnp.sin(ang)], axis=-1)
    z = jnp.pad(z, ((0, 0), (0, LANES - HY_EMB)))
    w1p = jnp.pad(w1.astype(F32), ((0, LANES - HY_EMB), (0, 0)))
    deltas = jnp.abs(jnp.linspace(math.log(HY_TARGET) / HY_SLOW, math.log(HY_TARGET) / HY_FAST,
                                  HY_WIDTH, dtype=F32))
    dl = jnp.tile(deltas, 2)[None, :]
    row = lambda a: a.astype(F32).reshape(1, -1)
    filt = pl.pallas_call(
        _hy_filter_kernel, out_shape=jax.ShapeDtypeStruct((S, 2 * HY_WIDTH), F32),
        compiler_params=pltpu.CompilerParams(vmem_limit_bytes=VMEM_LIMIT), name="hyena_filter",
    )(z, w1p, row(b1), row(f1), w2.astype(F32), row(b2), row(f2), w3.astype(F32), dl)
    tf = HY_FREQ_TILE
    return pl.pallas_call(
        _hy_spectrum_kernel, grid=(2, S // tf),
        in_specs=[pl.BlockSpec((1, tf, S), lambda c, f: (c, f, 0)),
                  pl.BlockSpec((S, 2 * HY_WIDTH), lambda c, f: (0, 0))],
        out_specs=pl.BlockSpec((1, tf, 2 * HY_WIDTH), lambda c, f: (c, f, 0)),
        out_shape=jax.ShapeDtypeStruct((2, S, 2 * HY_WIDTH), F32),
        compiler_params=_cparams("parallel", "parallel"), name="hyena_spectrum",
    )(dft, filt)


def _hyena_kernel(x_ref, cw_ref, cb_ref, bias_ref, f_ref, ft_ref, sp_ref, o_ref,
                  u_s, x0_s, acc_s):
    f = pl.program_id(1)
    S = x_ref.shape[1]
    W = HY_WIDTH
    tf = f_ref.shape[1]

    @pl.when(f == 0)
    def _():
        row = lax.broadcasted_iota(jnp.int32, (S, LANES), 0)
        cw = cw_ref[...]
        cb = cb_ref[...]

        def conv(t):
            sl = slice(t * LANES, (t + 1) * LANES)
            x = x_ref[0, :, sl].astype(F32)
            prev, nxt = _shift_rows(x, row, S)
            return prev * cw[0:1, sl] + x * cw[1:2, sl] + nxt * cw[2:3, sl] + cb[:, sl]

        nt = W // LANES
        for t in range(nt):
            sl = slice(t * LANES, (t + 1) * LANES)
            x0_s[:, sl] = conv(t)
            u_s[:, sl] = (conv(2 * nt + t) * conv(nt + t)).astype(BF16)
        acc_s[...] = jnp.zeros_like(acc_s)

    n = 2 * S
    sp_c = sp_ref[0]
    sp_s = sp_ref[1]
    s_re = sp_c[:, :W] + sp_c[:, W:]
    slot0 = (lax.broadcasted_iota(jnp.int32, (tf, W), 0) + f * tf) == 0
    wk = jnp.where(slot0, 1.0 / n, 2.0 / n)
    ca = s_re * wk
    cb2 = jnp.where(slot0, 0.0, (sp_s[:, :W] - sp_s[:, W:]) * wk)
    cd = jnp.where(slot0, (sp_s[:, :W] + sp_s[:, W:]) * wk, ca)

    u = u_s[...]
    u_re = _dot(f_ref[0], u)
    u_im = _dot(f_ref[1], u)
    y_re = (u_re * ca - u_im * cb2).astype(BF16)
    y_im = (u_re * cb2 + u_im * cd).astype(BF16)
    acc_s[...] += _dot(ft_ref[0], y_re) + _dot(ft_ref[1], y_im)

    @pl.when(f == pl.num_programs(1) - 1)
    def _():
        y = acc_s[...] + u_s[...].astype(F32) * bias_ref[...]
        o_ref[0] = (y * x0_s[...]).astype(o_ref.dtype)


def _hyena(hy, conv_w, conv_b, bias, dft_b, dft_tb, spec):
    B, S, _ = hy.shape
    W = HY_WIDTH
    tf = HY_FREQ_TILE
    full = lambda shp: pl.BlockSpec(shp, lambda b, f: (0,) * len(shp))
    return pl.pallas_call(
        _hyena_kernel, grid=(B, S // tf),
        in_specs=[pl.BlockSpec((1, S, 3 * W), lambda b, f: (b, 0, 0)),
                  full((3, 3 * W)), full((1, 3 * W)), full((1, W)),
                  pl.BlockSpec((2, tf, S), lambda b, f: (0, f, 0)),
                  pl.BlockSpec((2, S, tf), lambda b, f: (0, 0, f)),
                  pl.BlockSpec((2, tf, 2 * W), lambda b, f: (0, f, 0))],
        out_specs=pl.BlockSpec((1, S, W), lambda b, f: (b, 0, 0)),
        out_shape=jax.ShapeDtypeStruct((B, S, W), BF16),
        scratch_shapes=[pltpu.VMEM((S, W), BF16), pltpu.VMEM((S, W), F32), pltpu.VMEM((S, W), F32)],
        compiler_params=_cparams("parallel", "arbitrary"), name="hyena",
    )(hy, conv_w.astype(F32), conv_b.astype(F32).reshape(1, -1), bias.astype(F32).reshape(1, -1),
      dft_b, dft_tb, spec)


def kernel(x, norm_mix, norm_ffn, final_norm, ab_w_in, dn_conv_w, dn_a_log, dn_dt_bias, dn_norm_w, ab_w_out, cd_w_in, ret_log_decay, hy_conv_w, hy_conv_b, hy_w1, hy_b1, hy_f1, hy_w2, hy_b2, hy_f2, hy_w3, hy_bias, cd_w_out, ffn_w_in, ffn_conv_w, ffn_conv_b, ffn_w_out):
    B, S, D = x.shape
    T = B * S
    x2 = x.reshape(T, D)
    cos_t, sin_t = _rope_tables(S, DIL_HEAD_DIM)

    w = ab_w_in[0]
    W = DN_WIDTH
    n_gate = 4 * DN_HEADS
    w_ba = jnp.pad(w[:, 4 * W:4 * W + n_gate], ((0, 0), (0, LANES - n_gate)))
    qkv, z, ba, dil = _norm_proj(
        x2, norm_mix[0],
        [w[:, :3 * W].astype(BF16), w[:, 3 * W:4 * W].astype(BF16), w_ba.astype(BF16),
         w[:, 4 * W + n_gate:].astype(BF16)],
        [BF16, BF16, F32, BF16])
    y_a = _deltanet(qkv.reshape(B, S, -1), z.reshape(B, S, -1), ba.reshape(B, S, -1),
                    dn_conv_w[0], dn_a_log[0], dn_dt_bias[0], dn_norm_w[0])
    y_b = _dilated_attention(dil.reshape(B, S, -1), cos_t, sin_t)
    wo = ab_w_out[0].astype(BF16)
    x2 = _out_proj(x2, y_a.reshape(T, -1), y_b.reshape(T, -1), wo[:W], wo[W:])
    x3 = _conv_ffn(x2.reshape(B, S, D), norm_ffn[0], ffn_w_in[0], ffn_conv_w[0], ffn_conv_b[0],
                   ffn_w_out[0], final_norm, False)

    w = cd_w_in[0]
    c0, c1, c2 = 2 * RET_QK, 2 * RET_QK + RET_V, 2 * RET_QK + 2 * RET_V
    qk, v, gate, hy = _norm_proj(
        x3.reshape(T, D), norm_mix[1],
        [w[:, :c0].astype(BF16), w[:, c0:c1].astype(BF16), w[:, c1:c2].astype(BF16),
         w[:, c2:].astype(BF16)],
        [BF16, BF16, BF16, BF16])
    y_c = _retention(qk.reshape(B, S, -1), v.reshape(B, S, -1), gate.reshape(B, S, -1),
                     ret_log_decay[0], cos_t, sin_t)
    dft = _dft_matrix(S)
    spec = _hyena_spectrum(S, dft, hy_w1[0], hy_b1[0], hy_f1[0], hy_w2[0], hy_b2[0], hy_f2[0],
                           hy_w3[0])
    dft_b = dft.astype(BF16)
    y_d = _hyena(hy.reshape(B, S, -1), hy_conv_w[0], hy_conv_b[0], hy_bias[0],
                 dft_b, jnp.swapaxes(dft_b, 1, 2), spec)
    wo = cd_w_out[0].astype(BF16)
    x2 = _out_proj(x3.reshape(T, D), y_c.reshape(T, -1), y_d.reshape(T, -1), wo[:RET_V], wo[RET_V:])
    return _conv_ffn(x2.reshape(B, S, D), norm_ffn[1], ffn_w_in[1], ffn_conv_w[1], ffn_conv_b[1],
                     ffn_w_out[1], final_norm, True)
```

```python
import functools
import math

import numpy as np
import jax
import jax.numpy as jnp
from jax import lax
from jax.experimental import pallas as pl
from jax.experimental.pallas import tpu as pltpu

F32 = jnp.float32
BF16 = jnp.bfloat16
HIGHEST = lax.Precision.HIGHEST

NORM_EPS = 1e-6
ROPE_THETA = 10000.0
NEG_INF = -1e30

D_MODEL = 1024
DN_HEADS = 4
DN_HEAD_DIM = 128
DN_WIDTH = DN_HEADS * DN_HEAD_DIM
DN_CHUNK = 64
DN_CHAINS = 2 * DN_HEADS
DIL_HEADS = 8
DIL_HEAD_DIM = 64
DIL_WIDTH = DIL_HEADS * DIL_HEAD_DIM
DIL_PATTERNS = ((128, 1), (512, 4), (2048, 16))
DIL_QBLK = 128
DIL_UNROLL = 4
RET_HEADS = 4
RET_KEY_DIM = 64
RET_VAL_DIM = 128
RET_QK = RET_HEADS * RET_KEY_DIM
RET_V = RET_HEADS * RET_VAL_DIM
RET_CHUNK = 256
HY_WIDTH = 512
HY_EMB = 33
HY_ORDER = 64
HY_TARGET = 1e-2
HY_FAST = 0.3
HY_SLOW = 1.5
HY_FREQ_TILE = 256
D_FF = 2816
FF_TILE = 256

LANES = 128
VMEM_LIMIT = 56 * 1024 * 1024


def _cparams(*sem):
    return pltpu.CompilerParams(dimension_semantics=sem, vmem_limit_bytes=VMEM_LIMIT)


def _dot(a, b):
    return jnp.dot(a, b, preferred_element_type=F32)


def _dot_nt(a, b):
    return lax.dot_general(a, b, (((1,), (1,)), ((), ())), preferred_element_type=F32)


def _dot_tn(a, b):
    return lax.dot_general(a, b, (((0,), (0,)), ((), ())), preferred_element_type=F32)


def _silu(x):
    return x * (1.0 / (1.0 + jnp.exp(-x)))


def _shift_rows(x, row, n_rows):
    prev = jnp.where(row == 0, 0.0, pltpu.roll(x, 1, axis=0))
    nxt = jnp.where(row == n_rows - 1, 0.0, pltpu.roll(x, n_rows - 1, axis=0))
    return prev, nxt


def _proj_kernel(x_ref, nw_ref, *refs):
    n = len(refs) // 2
    x = x_ref[...]
    h = x * lax.rsqrt(jnp.mean(x * x, axis=-1, keepdims=True) + NORM_EPS) * nw_ref[...]
    hb = h.astype(BF16)
    for w_ref, o_ref in zip(refs[:n], refs[n:]):
        o_ref[...] = _dot(hb, w_ref[...]).astype(o_ref.dtype)


def _norm_proj(x2d, norm_w, weights, out_dtypes, tm=512):
    T, D = x2d.shape
    in_specs = [pl.BlockSpec((tm, D), lambda i: (i, 0)),
                pl.BlockSpec((1, D), lambda i: (0, 0))]
    in_specs += [pl.BlockSpec(w.shape, lambda i: (0, 0)) for w in weights]
    out_specs = [pl.BlockSpec((tm, w.shape[1]), lambda i: (i, 0)) for w in weights]
    out_shape = [jax.ShapeDtypeStruct((T, w.shape[1]), dt) for w, dt in zip(weights, out_dtypes)]
    return pl.pallas_call(
        _proj_kernel, grid=(T // tm,), in_specs=in_specs, out_specs=out_specs,
        out_shape=out_shape, compiler_params=_cparams("parallel"), name="norm_proj",
    )(x2d, norm_w.reshape(1, D), *weights)


def _outproj_kernel(x_ref, ya_ref, yb_ref, wa_ref, wb_ref, o_ref):
    o_ref[...] = x_ref[...] + _dot(ya_ref[...], wa_ref[...]) + _dot(yb_ref[...], wb_ref[...])


def _out_proj(x2d, ya, yb, wa, wb, tm=1024):
    T, D = x2d.shape
    row = lambda i: (i, 0)
    full = lambda i: (0, 0)
    return pl.pallas_call(
        _outproj_kernel, grid=(T // tm,),
        in_specs=[pl.BlockSpec((tm, D), row), pl.BlockSpec((tm, ya.shape[1]), row),
                  pl.BlockSpec((tm, yb.shape[1]), row), pl.BlockSpec(wa.shape, full),
                  pl.BlockSpec(wb.shape, full)],
        out_specs=pl.BlockSpec((tm, D), row),
        out_shape=jax.ShapeDtypeStruct((T, D), F32),
        compiler_params=_cparams("parallel"), name="out_proj",
    )(x2d, ya, yb, wa, wb)


def _ffn_kernel(x_ref, nw_ref, wg_ref, wu_ref, cw_ref, cb_ref, wo_ref, fw_ref, o_ref, h_ref,
                *, final_norm):
    j = pl.program_id(1)
    S = x_ref.shape[1]

    @pl.when(j == 0)
    def _():
        x = x_ref[0]
        h = x * lax.rsqrt(jnp.mean(x * x, axis=-1, keepdims=True) + NORM_EPS) * nw_ref[...]
        h_ref[...] = h.astype(BF16)
        o_ref[0] = x

    hb = h_ref[...]
    gate = _dot(hb, wg_ref[...])
    up = _dot(hb, wu_ref[...])
    row = lax.broadcasted_iota(jnp.int32, gate.shape, 0)
    prev, nxt = _shift_rows(gate, row, S)
    cw = cw_ref[...]
    g = prev * cw[0:1] + gate * cw[1:2] + nxt * cw[2:3] + cb_ref[...]
    act = (_silu(g) * up).astype(BF16)
    o_ref[0] += _dot(act, wo_ref[...])

    if final_norm:
        @pl.when(j == pl.num_programs(1) - 1)
        def _():
            y = o_ref[0]
            o_ref[0] = y * lax.rsqrt(jnp.mean(y * y, axis=-1, keepdims=True) + NORM_EPS) * fw_ref[...]


def _conv_ffn(x, norm_w, w_in, conv_w, conv_b, w_out, final_w, final_norm):
    B, S, D = x.shape
    tf = FF_TILE
    nj = D_FF // tf
    wg = w_in[:, :D_FF].astype(BF16)
    wu = w_in[:, D_FF:].astype(BF16)
    wo = w_out.astype(BF16)
    return pl.pallas_call(
        functools.partial(_ffn_kernel, final_norm=final_norm), grid=(B, nj),
        in_specs=[pl.BlockSpec((1, S, D), lambda b, j: (b, 0, 0)),
                  pl.BlockSpec((1, D), lambda b, j: (0, 0)),
                  pl.BlockSpec((D, tf), lambda b, j: (0, j)),
                  pl.BlockSpec((D, tf), lambda b, j: (0, j)),
                  pl.BlockSpec((3, tf), lambda b, j: (0, j)),
                  pl.BlockSpec((1, tf), lambda b, j: (0, j)),
                  pl.BlockSpec((tf, D), lambda b, j: (j, 0)),
                  pl.BlockSpec((1, D), lambda b, j: (0, 0))],
        out_specs=pl.BlockSpec((1, S, D), lambda b, j: (b, 0, 0)),
        out_shape=jax.ShapeDtypeStruct((B, S, D), F32),
        scratch_shapes=[pltpu.VMEM((S, D), BF16)],
        compiler_params=_cparams("parallel", "arbitrary"), name="conv_ffn",
    )(x, norm_w.reshape(1, D), wg, wu, conv_w, conv_b.reshape(1, D_FF), wo, final_w.reshape(1, D))


def _deltanet_kernel(qkv_ref, z_ref, ba_ref, cw_ref, gp_ref, nw_ref, o_ref,
                     q_s, k_s, v_s, pk_s, o_s, u_s, w_s, in_s, st_s):
    S = qkv_ref.shape[1]
    W, E, C, H = DN_WIDTH, DN_HEAD_DIM, DN_CHUNK, DN_HEADS
    n_chunks = S // C
    row = lax.broadcasted_iota(jnp.int32, (S, LANES), 0)
    lane = lax.broadcasted_iota(jnp.int32, (S, LANES), 1)

    cw = cw_ref[...]
    for t in range(3 * W // LANES):
        sl = slice(t * LANES, (t + 1) * LANES)
        x = qkv_ref[0, :, sl].astype(F32)
        prev, nxt = _shift_rows(x, row, S)
        y = _silu(prev * cw[0:1, sl] + x * cw[1:2, sl] + nxt * cw[2:3, sl])
        h = t % H
        hs = slice(h * E, (h + 1) * E)
        if t < H:
            y = y * lax.rsqrt(jnp.sum(y * y, axis=-1, keepdims=True) + 1e-6) * (E ** -0.5)
            q_s[:, hs] = y.astype(BF16)
        elif t < 2 * H:
            y = y * lax.rsqrt(jnp.sum(y * y, axis=-1, keepdims=True) + 1e-6)
            k_s[:, hs] = y.astype(BF16)
        else:
            v_s[:, hs] = y.astype(BF16)

    ba = ba_ref[0]
    beta = 1.0 / (1.0 + jnp.exp(-ba))
    xg = ba + gp_ref[1:2, :]
    g = gp_ref[0:1, :] * (jnp.maximum(xg, 0.0) + jnp.log1p(jnp.exp(-jnp.abs(xg))))
    pos = row & (C - 1)
    pre = g
    suf = g
    sh = 1
    while sh < C:
        pre = pre + jnp.where(pos >= sh, pltpu.roll(pre, sh, axis=0), 0.0)
        suf = suf + jnp.where(pos < C - sh, pltpu.roll(suf, S - sh, axis=0), 0.0)
        sh *= 2
    fwd = lane < 8 + H
    gc = jnp.where(fwd, pre, suf)
    glc = jnp.where(fwd, suf, pre) - g
    pk = jnp.where(lane < 8, beta,
                   jnp.where(lane < 16, gc,
                             jnp.where(lane < 24, pltpu.roll(glc, 8, axis=1),
                                       pltpu.roll(gc + glc, 16, axis=1))))
    pk_s[...] = pk

    st_s[...] = jnp.zeros_like(st_s)
    o_s[...] = jnp.zeros_like(o_s)

    ri =lax.broadcasted_iota(jnp.int32, (C, C), 0)
    ci = lax.broadcasted_iota(jnp.int32, (C, C), 1)
    eye = (ri == ci).astype(F32)

    heads = [slice(h * E, (h + 1) * E) for h in range(H)]
    chains = [(d, h) for d in range(2) for h in range(H)]


    def pre_body(n, carry):
        rows = pl.ds(pl.multiple_of(n * C, C), C)
        pkc = pk_s[rows, :]
        tr = jnp.transpose(pkc)
        k_c = [k_s[rows, hs] for hs in heads]
        v_c = [v_s[rows, hs] for hs in heads]
        s2 = [_dot_nt(jnp.concatenate([q_s[rows, hs], k], axis=0), k)
              for hs, k in zip(heads, k_c)]
        p_b, t_inv, b_row, bg_row = [], [], [], []
        for d, h in chains:
            c = d * H + h
            causal = (ri >= ci) if d == 0 else (ri <= ci)
            strict = (ri > ci) if d == 0 else (ri < ci)
            beta_col = pkc[:, c:c + 1]
            gc_col = pkc[:, 8 + c:9 + c]
            beta_row = tr[c:c + 1, :]
            gc_row = tr[8 + c:9 + c, :]
            dmat = jnp.where(causal, jnp.exp(jnp.where(causal, gc_col - gc_row, 0.0)), 0.0)
            in_s[d, rows, h * C:(h + 1) * C] = (s2[h][:C] * dmat).astype(BF16)
            p = -jnp.where(strict, s2[h][C:] * beta_col * dmat, 0.0)
            p_b.append(p.astype(BF16))
            t_inv.append(eye + p)
            b_row.append(beta_row)
            bg_row.append(beta_row * jnp.exp(gc_row))
        pw = [_dot(pb, pb) for pb in p_b]
        for it in range(1, 6):
            pwb = [x.astype(BF16) for x in pw]
            if it < 5:
                zz = [_dot(a, jnp.concatenate([a, t.astype(BF16)], axis=1))
                      for a, t in zip(pwb, t_inv)]
                pw = [z[:, :C] for z in zz]
                t_inv = [t + z[:, C:] for t, z in zip(t_inv, zz)]
            else:
                t_inv = [t + _dot(a, t.astype(BF16)) for a, t in zip(pwb, t_inv)]
        us = [_dot((t * br).astype(BF16), v_c[h]) for (d, h), t, br in zip(chains, t_inv, b_row)]
        ws = [_dot((t * br).astype(BF16), k_c[h]) for (d, h), t, br in zip(chains, t_inv, bg_row)]
        for (d, h), u, w in zip(chains, us, ws):
            u_s[d, rows, heads[h]] = u.astype(BF16)
            w_s[d, rows, heads[h]] = w.astype(BF16)
        return carry

    lax.fori_loop(0, n_chunks, pre_body, 0)

    def rec_body(n, carry):
        rows, ex = [], []
        for d in range(2):
            nd = n if d == 0 else n_chunks - 1 - n
            rows.append(pl.ds(pl.multiple_of(nd * C, C), C))
            ex.append(jnp.exp(pk_s[rows[d], :]))
        st = [st_s[d * H + h] for d, h in chains]
        wq = [_dot(jnp.concatenate([w_s[d, rows[d], heads[h]], q_s[rows[d], heads[h]]], axis=0),
                   s.astype(BF16)) for (d, h), s in zip(chains, st)]
        for (d, h), s, x in zip(chains, st, wq):
            c = d * H + h
            eg_col = ex[d][:, 8 + c:9 + c]
            eglc_col = ex[d][:, 16 + c:17 + c]
            etot_col = ex[d][:, 24 + c:25 + c]
            v_new = u_s[d, rows[d], heads[h]].astype(F32) - x[:C]
            o = eg_col * x[C:] + _dot(in_s[d, rows[d], h * C:(h + 1) * C], v_new.astype(BF16))
            upd = _dot_tn(k_s[rows[d], heads[h]], (eglc_col * v_new).astype(BF16))
            st_s[c] = jnp.concatenate([etot_col, etot_col], axis=0) * s + upd
            o_s[rows[d], heads[h]] += o
        return carry

    lax.fori_loop(0, n_chunks, rec_body, 0)

    nw = nw_ref[...]
    for h in range(H):
        hs = slice(h * E, (h + 1) * E)
        o = o_s[:, hs]
        o = o * lax.rsqrt(jnp.mean(o * o, axis=-1, keepdims=True) + NORM_EPS) * nw
        o_ref[0, :, hs] = (o * _silu(z_ref[0, :, hs].astype(F32))).astype(o_ref.dtype)


def _deltanet(qkv, z, ba, conv_w, a_log, dt_bias, norm_w):
    B, S, _ = qkv.shape
    W, E = DN_WIDTH, DN_HEAD_DIM
    gp = jnp.zeros((2, LANES), F32)
    gp = gp.at[0, 8:16].set(-jnp.exp(a_log.astype(F32)).reshape(-1))
    gp = gp.at[1, 8:16].set(dt_bias.astype(F32).reshape(-1))
    bspec = lambda n: pl.BlockSpec((1, S, n), lambda b: (b, 0, 0))
    full = lambda shp: pl.BlockSpec(shp, lambda b: (0,) * len(shp))
    return pl.pallas_call(
        _deltanet_kernel, grid=(B,),
        in_specs=[bspec(3 * W), bspec(W), bspec(LANES), full((3, 3 * W)), full((2, LANES)),
                  full((1, E))],
        out_specs=bspec(W),
        out_shape=jax.ShapeDtypeStruct((B, S, W), BF16),
        scratch_shapes=[pltpu.VMEM((S, W), BF16), pltpu.VMEM((S, W), BF16), pltpu.VMEM((S, W), BF16),
                        pltpu.VMEM((S, LANES), F32), pltpu.VMEM((S, W), F32),
                        pltpu.VMEM((2, S, W), BF16), pltpu.VMEM((2, S, W), BF16),
                        pltpu.VMEM((2, S, DN_HEADS * DN_CHUNK), BF16),
                        pltpu.VMEM((DN_CHAINS, E, E), F32)],
        compiler_params=_cparams("parallel"), name="deltanet",
    )(qkv, z, ba, conv_w.astype(F32), gp, norm_w.reshape(1, E).astype(F32))


def _rope_tables(S, E):
    inv = ROPE_THETA ** (-jnp.arange(0, E, 2, dtype=F32) / E)
    ang = jnp.arange(S, dtype=F32)[:, None] * inv[None, :]
    cos, sin = jnp.cos(ang), jnp.sin(ang)
    cos_t = jnp.tile(jnp.concatenate([cos, cos], axis=-1), (1, LANES // E))
    sin_t = jnp.tile(jnp.concatenate([-sin, sin], axis=-1), (1, LANES // E))
    return cos_t, sin_t


def _rope_tile(x, cos_t, sin_t, lane):
    half = DIL_HEAD_DIM // 2
    partner = jnp.where((lane & (DIL_HEAD_DIM - 1)) < half,
                        pltpu.roll(x, LANES - half, axis=1), pltpu.roll(x, half, axis=1))
    return x * cos_t + partner * sin_t


def _dilated_kernel(q_ref, k_ref, v_ref, cos_ref, sin_ref, o_ref,
                    q_s, k_s, v_s, op_s, lse_s):
    S = q_ref.shape[1]
    E = DIL_HEAD_DIM
    QB = DIL_QBLK
    lane = lax.broadcasted_iota(jnp.int32, (S, LANES), 1)
    cos_t = cos_ref[...]
    sin_t = sin_ref[...]
    q_s[...] = _rope_tile(q_ref[0].astype(F32), cos_t, sin_t, lane) * (E ** -0.5)
    k_s[...] = _rope_tile(k_ref[0].astype(F32), cos_t, sin_t, lane)
    v_s[...] = v_ref[0].astype(F32)
    head_lanes = [slice(hh * E, (hh + 1) * E) for hh in range(LANES // E)]

    for p, (window, dil) in enumerate(DIL_PATTERNS):
        half = window // (2 * dil)
        L = S // dil
        nb = L // QB
        KW = min(QB + 2 * half, L)
        relm = (lax.broadcasted_iota(jnp.int32, (QB, KW), 1)
                - lax.broadcasted_iota(jnp.int32, (QB, KW), 0))

        def block(i, carry, dil=dil, half=half, L=L, nb=nb, KW=KW, relm=relm, p=p):
            qrows, valid, vbs, scores = [], [], [], []
            for uu in range(DIL_UNROLL):
                it = i * DIL_UNROLL + uu
                r = it // nb
                n = it % nb
                q0 = n * QB
                k0 = jnp.clip(q0 - half, 0, L - KW)
                if dil == 1:
                    qr = pl.ds(pl.multiple_of(q0, QB), QB)
                    kr = pl.ds(pl.multiple_of(k0, half), KW)
                else:
                    qr = pl.ds(r + dil * q0, QB, stride=dil)
                    kr = pl.ds(r + dil * k0, KW, stride=dil)
                qrows.append(qr)
                valid.append(jnp.abs(relm + (k0 - q0)) <= half)
                qb = q_s[qr, :].astype(BF16)
                kb = k_s[kr, :].astype(BF16)
                vbs.append(v_s[kr, :].astype(BF16))
                for hs in head_lanes:
                    scores.append(_dot_nt(qb[:, hs], kb[:, hs]))
            es, dens, lses = [], [], []
            for j, s in enumerate(scores):
                s = jnp.where(valid[j // len(head_lanes)], s, NEG_INF)
                m = jnp.max(s, axis=-1, keepdims=True)
                e = jnp.exp(s - m)
                den = jnp.sum(e, axis=-1, keepdims=True)
                es.append(e.astype(BF16))
                dens.append(den)
                lses.append(jnp.broadcast_to(m + jnp.log(den), (QB, E)))
            nh = len(head_lanes)
            outs = [_dot(e, vbs[j // nh][:, head_lanes[j % nh]]) * (1.0 / dens[j])
                    for j, e in enumerate(es)]
            for uu in range(DIL_UNROLL):
                op_s[p, qrows[uu], :] = jnp.concatenate(outs[uu * nh:(uu + 1) * nh], axis=1)
                lse_s[p, qrows[uu], :] = jnp.concatenate(lses[uu * nh:(uu + 1) * nh], axis=1)
            return carry

        lax.fori_loop(0, dil * nb // DIL_UNROLL, block, 0)

    l0, l1, l2 = lse_s[0], lse_s[1], lse_s[2]
    mx = jnp.maximum(jnp.maximum(l0, l1), l2)
    w0, w1, w2 = jnp.exp(l0 - mx), jnp.exp(l1 - mx), jnp.exp(l2 - mx)
    out = (w0 * op_s[0] + w1 * op_s[1] + w2 * op_s[2]) * (1.0 / (w0 + w1 + w2))
    o_ref[0] = out.astype(o_ref.dtype)


def _dilated_attention(qkv, cos_t, sin_t):
    B, S, _ = qkv.shape
    nt = DIL_WIDTH // LANES
    tile = lambda off: pl.BlockSpec((1, S, LANES), lambda b, t: (b, 0, t + off))
    tab = pl.BlockSpec((S, LANES), lambda b, t: (0, 0))
    return pl.pallas_call(
        _dilated_kernel, grid=(B, nt),
        in_specs=[tile(0), tile(nt), tile(2 * nt), tab, tab],
        out_specs=tile(0),
        out_shape=jax.ShapeDtypeStruct((B, S, DIL_WIDTH), BF16),
        scratch_shapes=[pltpu.VMEM((S, LANES), F32)] * 3
        + [pltpu.VMEM((len(DIL_PATTERNS), S, LANES), F32)] * 2,
        compiler_params=_cparams("parallel", "parallel"), name="dilated_attention",
    )(qkv, qkv, qkv, cos_t, sin_t)


def _retention_kernel(lg_ref, qk_ref, v_ref, g_ref, cos_ref, sin_ref, o_ref,
                      qk_s, d_s, kv_s, o_s):
    S = qk_ref.shape[1]
    H, EK, EV, C = RET_HEADS, RET_KEY_DIM, RET_VAL_DIM, RET_CHUNK
    n_chunks = S // C
    lane = lax.broadcasted_iota(jnp.int32, (S, LANES), 1)
    cos_t = cos_ref[...]
    sin_t = sin_ref[...]
    for t in range(2 * RET_QK // LANES):
        sl = slice(t * LANES, (t + 1) * LANES)
        y = _rope_tile(qk_ref[0, :, sl].astype(F32), cos_t, sin_t, lane)
        qk_s[:, sl] = y * (EK ** -0.5) if t < RET_QK // LANES else y

    rel = (lax.broadcasted_iota(jnp.int32, (C, C), 0)
           - lax.broadcasted_iota(jnp.int32, (C, C), 1)).astype(F32)
    idx = lax.broadcasted_iota(jnp.int32, (C, EK), 0).astype(F32)
    for h in range(H):
        lgf = lg_ref[0, h]
        lgb = lg_ref[1, h]
        d_s[h] = jnp.exp(lgf * jnp.maximum(rel, 0.0) + lgb * jnp.maximum(-rel, 0.0))
        k_dec = jnp.concatenate([jnp.exp(lgf * (C - 1 - idx)), jnp.exp(lgb * idx)], axis=1)
        q_dec = jnp.concatenate([jnp.exp(lgf * (idx + 1)), jnp.exp(lgb * (C - idx))], axis=1)
        qs = slice(h * EK, (h + 1) * EK)
        ks = slice(RET_QK + h * EK, RET_QK + (h + 1) * EK)
        vs = slice(h * EV, (h + 1) * EV)
        for n in range(n_chunks):
            rows = slice(n * C, (n + 1) * C)
            k_c = qk_s[rows, ks]
            k2 = (jnp.concatenate([k_c, k_c], axis=1) * k_dec).astype(BF16)
            kv_s[n] = _dot_tn(k2, v_ref[0, rows, vs])
        fdec = jnp.exp(lgf * C)
        bdec = jnp.exp(lgb * C)
        st = jnp.zeros((EK, EV), F32)
        for n in range(n_chunks):
            inc = kv_s[n, :EK, :]
            kv_s[n, :EK, :] = st
            st = st * fdec + inc
        st = jnp.zeros((EK, EV), F32)
        for n in range(n_chunks - 1, -1, -1):
            inc = kv_s[n, EK:, :]
            kv_s[n, EK:, :] = st
            st = st * bdec + inc
        for n in range(n_chunks):
            rows = slice(n * C, (n + 1) * C)
            q_c = qk_s[rows, qs]
            k_c = qk_s[rows, ks]
            sc = (_dot_nt(q_c.astype(BF16), k_c.astype(BF16)) * d_s[h]).astype(BF16)
            q2 = (jnp.concatenate([q_c, q_c], axis=1) * q_dec).astype(BF16)
            o_s[rows, vs] = _dot(sc, v_ref[0, rows, vs]) + _dot(q2, kv_s[n].astype(BF16))

    for h in range(H):
        vs = slice(h * EV, (h + 1) * EV)
        o = o_s[:, vs]
        o = o * lax.rsqrt(jnp.mean(o * o, axis=-1, keepdims=True) + NORM_EPS)
        o_ref[0, :, vs] = (o * _silu(g_ref[0, :, vs].astype(F32))).astype(o_ref.dtype)


def _retention(qk, v, gate, log_decay, cos_t, sin_t):
    B, S, _ = qk.shape
    H, EK, EV, C = RET_HEADS, RET_KEY_DIM, RET_VAL_DIM, RET_CHUNK
    bspec = lambda n: pl.BlockSpec((1, S, n), lambda b: (b, 0, 0))
    tab = pl.BlockSpec((S, LANES), lambda b: (0, 0))
    return pl.pallas_call(
        _retention_kernel, grid=(B,),
        in_specs=[pl.BlockSpec(memory_space=pltpu.SMEM), bspec(2 * RET_QK), bspec(RET_V),
                  bspec(RET_V), tab, tab],
        out_specs=bspec(RET_V),
        out_shape=jax.ShapeDtypeStruct((B, S, RET_V), BF16),
        scratch_shapes=[pltpu.VMEM((S, 2 * RET_QK), F32), pltpu.VMEM((H, C, C), F32),
                        pltpu.VMEM((S // C, 2 * EK, EV), F32), pltpu.VMEM((S, RET_V), F32)],
        compiler_params=_cparams("parallel"), name="retention",
    )(log_decay.astype(F32), qk, v, gate, cos_t, sin_t)


def _dft_matrix(S):
    n = 2 * S
    k = jnp.arange(S, dtype=jnp.int32)[:, None]
    t = jnp.arange(S, dtype=jnp.int32)[None, :]
    ang = ((k * t) % n).astype(F32) * (2.0 * math.pi / n)
    cosb = jnp.cos(ang)
    sinb = -jnp.sin(ang)
    nyq = jnp.where(t % 2 == 0, 1.0, -1.0).astype(F32)
    sinb = jnp.where(k == 0, nyq, sinb)
    return jnp.stack([cosb, sinb])


def _hy_filter_kernel(z_ref, w1_ref, b1_ref, f1_ref, w2_ref, b2_ref, f2_ref, w3_ref, dl_ref, o_ref):
    dot = functools.partial(jnp.dot, preferred_element_type=F32, precision=HIGHEST)
    z = z_ref[...]
    hid = jnp.sin(f1_ref[...] * (dot(z, w1_ref[...]) + b1_ref[...]))
    hid = jnp.sin(f2_ref[...] * (dot(hid, w2_ref[...]) + b2_ref[...]))
    o_ref[...] = dot(hid, w3_ref[...]) * jnp.exp(-z[:, 0:1] * dl_ref[...])


def _hy_spectrum_kernel(f_ref, h_ref, o_ref):
    o_ref[0] = jnp.dot(f_ref[0], h_ref[...], preferred_element_type=F32, precision=HIGHEST)


def _hyena_spectrum(S, dft, w1, b1, f1, w2, b2, f2, w3):
    t = jnp.linspace(0.0, 1.0, S, dtype=F32)[:, None]
    bands = (HY_EMB - 1) // 2
    wv = 2.0 * math.pi * jnp.arange(S, dtype=F32) / S
    fr = jnp.linspace(1e-4, bands - 1, bands, dtype=F32)
    ang = wv[:, None] * fr[None, :]
    z = jnp.concatenate([t, jnp.cos(ang), -jnp.sin(ang)], axis=-1)
    z = jnp.pad(z, ((0, 0), (0, LANES - HY_EMB)))
    w1p = jnp.pad(w1.astype(F32), ((0, LANES - HY_EMB), (0, 0)))
    deltas = jnp.abs(jnp.linspace(math.log(HY_TARGET) / HY_SLOW, math.log(HY_TARGET) / HY_FAST,
                                  HY_WIDTH, dtype=F32))
    dl = jnp.tile(deltas, 2)[None, :]
    row = lambda a: a.astype(F32).reshape(1, -1)
    filt = pl.pallas_call(
        _hy_filter_kernel, out_shape=jax.ShapeDtypeStruct((S, 2 * HY_WIDTH), F32),
        compiler_params=pltpu.CompilerParams(vmem_limit_bytes=VMEM_LIMIT), name="hyena_filter",
    )(z, w1p, row(b1), row(f1), w2.astype(F32), row(b2), row(f2), w3.astype(F32), dl)
    tf = HY_FREQ_TILE
    return pl.pallas_call(
        _hy_spectrum_kernel, grid=(2, S // tf),
        in_specs=[pl.BlockSpec((1, tf, S), lambda c, f: (c, f, 0)),
                  pl.BlockSpec((S, 2 * HY_WIDTH), lambda c, f: (0, 0))],
        out_specs=pl.BlockSpec((1, tf, 2 * HY_WIDTH), lambda c, f: (c, f, 0)),
        out_shape=jax.ShapeDtypeStruct((2, S, 2 * HY_WIDTH), F32),
        compiler_params=_cparams("parallel", "parallel"), name="hyena_spectrum",
    )(dft, filt)


def _hyena_kernel(x_ref, cw_ref, cb_ref, bias_ref, f_ref, ft_ref, sp_ref, o_ref,
                  u_s, x0_s, acc_s):
    f = pl.program_id(1)
    S = x_ref.shape[1]
    W = HY_WIDTH
    tf = f_ref.shape[1]

    @pl.when(f == 0)
    def _():
        row = lax.broadcasted_iota(jnp.int32, (S, LANES), 0)
        cw = cw_ref[...]
        cb = cb_ref[...]

        def conv(t):
            sl = slice(t * LANES, (t + 1) * LANES)
            x = x_ref[0, :, sl].astype(F32)
            prev, nxt = _shift_rows(x, row, S)
            return prev * cw[0:1, sl] + x * cw[1:2, sl] + nxt * cw[2:3, sl] + cb[:, sl]

        nt = W // LANES
        for t in range(nt):
            sl = slice(t * LANES, (t + 1) * LANES)
            x0_s[:, sl] = conv(t)
            u_s[:, sl] = (conv(2 * nt + t) * conv(nt + t)).astype(BF16)
        acc_s[...] = jnp.zeros_like(acc_s)

    n = 2 * S
    sp_c = sp_ref[0]
    sp_s = sp_ref[1]
    s_re = sp_c[:, :W] + sp_c[:, W:]
    slot0 = (lax.broadcasted_iota(jnp.int32, (tf, W), 0) + f * tf) == 0
    wk = jnp.where(slot0, 1.0 / n, 2.0 / n)
    ca = s_re * wk
    cb2 = jnp.where(slot0, 0.0, (sp_s[:, :W] - sp_s[:, W:]) * wk)
    cd = jnp.where(slot0, (sp_s[:, :W] + sp_s[:, W:]) * wk, ca)

    u = u_s[...]
    u_re = _dot(f_ref[0], u)
    u_im = _dot(f_ref[1], u)
    y_re = (u_re * ca - u_im * cb2).astype(BF16)
    y_im = (u_re * cb2 + u_im * cd).astype(BF16)
    acc_s[...] += _dot(ft_ref[0], y_re) + _dot(ft_ref[1], y_im)

    @pl.when(f == pl.num_programs(1) - 1)
    def _():
        y = acc_s[...] + u_s[...].astype(F32) * bias_ref[...]
        o_ref[0] = (y * x0_s[...]).astype(o_ref.dtype)


def _hyena(hy, conv_w, conv_b, bias, dft_b, dft_tb, spec):
    B, S, _ = hy.shape
    W = HY_WIDTH
    tf = HY_FREQ_TILE
    full = lambda shp: pl.BlockSpec(shp, lambda b, f: (0,) * len(shp))
    return pl.pallas_call(
        _hyena_kernel, grid=(B, S // tf),
        in_specs=[pl.BlockSpec((1, S, 3 * W), lambda b, f: (b, 0, 0)),
                  full((3, 3 * W)), full((1, 3 * W)), full((1, W)),
                  pl.BlockSpec((2, tf, S), lambda b, f: (0, f, 0)),
                  pl.BlockSpec((2, S, tf), lambda b, f: (0, 0, f)),
                  pl.BlockSpec((2, tf, 2 * W), lambda b, f: (0, f, 0))],
        out_specs=pl.BlockSpec((1, S, W), lambda b, f: (b, 0, 0)),
        out_shape=jax.ShapeDtypeStruct((B, S, W), BF16),
        scratch_shapes=[pltpu.VMEM((S, W), BF16), pltpu.VMEM((S, W), F32), pltpu.VMEM((S, W), F32)],
        compiler_params=_cparams("parallel", "arbitrary"), name="hyena",
    )(hy, conv_w.astype(F32), conv_b.astype(F32).reshape(1, -1), bias.astype(F32).reshape(1, -1),
      dft_b, dft_tb, spec)


def kernel(x, norm_mix, norm_ffn, final_norm, ab_w_in, dn_conv_w, dn_a_log, dn_dt_bias, dn_norm_w, ab_w_out, cd_w_in, ret_log_decay, hy_conv_w, hy_conv_b, hy_w1, hy_b1, hy_f1, hy_w2, hy_b2, hy_f2, hy_w3, hy_bias, cd_w_out, ffn_w_in, ffn_conv_w, ffn_conv_b, ffn_w_out):
    B, S, D = x.shape
    T = B * S
    x2 = x.reshape(T, D)
    cos_t, sin_t = _rope_tables(S, DIL_HEAD_DIM)

    w = ab_w_in[0]
    W = DN_WIDTH
    n_gate = 4 * DN_HEADS
    w_ba = jnp.pad(w[:, 4 * W:4 * W + n_gate], ((0, 0), (0, LANES - n_gate)))
    qkv, z, ba, dil = _norm_proj(
        x2, norm_mix[0],
        [w[:, :3 * W].astype(BF16), w[:, 3 * W:4 * W].astype(BF16), w_ba.astype(BF16),
         w[:, 4 * W + n_gate:].astype(BF16)],
        [BF16, BF16, F32, BF16])
    y_a = _deltanet(qkv.reshape(B, S, -1), z.reshape(B, S, -1), ba.reshape(B, S, -1),
                    dn_conv_w[0], dn_a_log[0], dn_dt_bias[0], dn_norm_w[0])
    y_b = _dilated_attention(dil.reshape(B, S, -1), cos_t, sin_t)
    wo = ab_w_out[0].astype(BF16)
    x2 = _out_proj(x2, y_a.reshape(T, -1), y_b.reshape(T, -1), wo[:W], wo[W:])
    x3 = _conv_ffn(x2.reshape(B, S, D), norm_ffn[0], ffn_w_in[0], ffn_conv_w[0], ffn_conv_b[0],
                   ffn_w_out[0], final_norm, False)

    w = cd_w_in[0]
    c0, c1, c2 = 2 * RET_QK, 2 * RET_QK + RET_V, 2 * RET_QK + 2 * RET_V
    qk, v, gate, hy = _norm_proj(
        x3.reshape(T, D), norm_mix[1],
        [w[:, :c0].astype(BF16), w[:, c0:c1].astype(BF16), w[:, c1:c2].astype(BF16),
         w[:, c2:].astype(BF16)],
        [BF16, BF16, BF16, BF16])
    y_c = _retention(qk.reshape(B, S, -1), v.reshape(B, S, -1), gate.reshape(B, S, -1),
                     ret_log_decay[0], cos_t, sin_t)
    dft = _dft_matrix(S)
    spec = _hyena_spectrum(S, dft, hy_w1[0], hy_b1[0], hy_f1[0], hy_w2[0], hy_b2[0], hy_f2[0],
                           hy_w3[0])
    dft_b = dft.astype(BF16)
    y_d = _hyena(hy.reshape(B, S, -1), hy_conv_w[0], hy_conv_b[0], hy_bias[0],
                 dft_b, jnp.swapaxes(dft_b, 1, 2), spec)
    wo = cd_w_out[0].astype(BF16)
    x2 = _out_proj(x3.reshape(T, D), y_c.reshape(T, -1), y_d.reshape(T, -1), wo[:RET_V], wo[RET_V:])
    return _conv_ffn(x2.reshape(B, S, D), norm_ffn[1], ffn_w_in[1], ffn_conv_w[1], ffn_conv_b[1],
                     ffn_w_out[1], final_norm, True)
```

```python
import functools
import math

import numpy as np
import jax
import jax.numpy as jnp
from jax import lax
from jax.experimental import pallas as pl
from jax.experimental.pallas import tpu as pltpu

F32 = jnp.float32
BF16 = jnp.bfloat16
HIGHEST = lax.Precision.HIGHEST

NORM_EPS = 1e-6
ROPE_THETA = 10000.0
NEG_INF = -1e30

D_MODEL = 1024
DN_HEADS = 4
DN_HEAD_DIM = 128
DN_WIDTH = DN_HEADS * DN_HEAD_DIM
DN_CHUNK = 64
DN_CHAINS = 2 * DN_HEADS
DN_PRE_CHUNKS = 4
DIL_HEADS = 8
DIL_HEAD_DIM = 64
DIL_WIDTH = DIL_HEADS * DIL_HEAD_DIM
DIL_PATTERNS = ((128, 1), (512, 4), (2048, 16))
DIL_QBLK = 128
DIL_UNROLL = 8
RET_HEADS = 4
RET_KEY_DIM = 64
RET_VAL_DIM = 128
RET_QK = RET_HEADS * RET_KEY_DIM
RET_V = RET_HEADS * RET_VAL_DIM
RET_CHUNK = 256
HY_WIDTH = 512
HY_EMB = 33
HY_ORDER = 64
HY_TARGET = 1e-2
HY_FAST = 0.3
HY_SLOW = 1.5
HY_FREQ_TILE = 256
D_FF = 2816
FF_TILE = 256

LANES = 128
VMEM_LIMIT = 56 * 1024 * 1024


def _cparams(*sem):
    return pltpu.CompilerParams(dimension_semantics=sem, vmem_limit_bytes=VMEM_LIMIT)


def _dot(a, b):
    return jnp.dot(a, b, preferred_element_type=F32)


def _dot_nt(a, b):
    return lax.dot_general(a, b, (((1,), (1,)), ((), ())), preferred_element_type=F32)


def _dot_tn(a, b):
    return lax.dot_general(a, b, (((0,), (0,)), ((), ())), preferred_element_type=F32)


def _silu(x):
    return x * (1.0 / (1.0 + jnp.exp(-x)))


def _shift_rows(x, row, n_rows):
    prev = jnp.where(row == 0, 0.0, pltpu.roll(x, 1, axis=0))
    nxt = jnp.where(row == n_rows - 1, 0.0, pltpu.roll(x, n_rows - 1, axis=0))
    return prev, nxt


def _proj_kernel(x_ref, nw_ref, *refs):
    n = len(refs) // 2
    x = x_ref[...]
    h = x * lax.rsqrt(jnp.mean(x * x, axis=-1, keepdims=True) + NORM_EPS) * nw_ref[...]
    hb = h.astype(BF16)
    for w_ref, o_ref in zip(refs[:n], refs[n:]):
        o_ref[...] = _dot(hb, w_ref[...]).astype(o_ref.dtype)


def _norm_proj(x2d, norm_w, weights, out_dtypes, tm=512):
    T, D = x2d.shape
    in_specs = [pl.BlockSpec((tm, D), lambda i: (i, 0)),
                pl.BlockSpec((1, D), lambda i: (0, 0))]
    in_specs += [pl.BlockSpec(w.shape, lambda i: (0, 0)) for w in weights]
    out_specs = [pl.BlockSpec((tm, w.shape[1]), lambda i: (i, 0)) for w in weights]
    out_shape = [jax.ShapeDtypeStruct((T, w.shape[1]), dt) for w, dt in zip(weights, out_dtypes)]
    return pl.pallas_call(
        _proj_kernel, grid=(T // tm,), in_specs=in_specs, out_specs=out_specs,
        out_shape=out_shape, compiler_params=_cparams("parallel"), name="norm_proj",
    )(x2d, norm_w.reshape(1, D), *weights)


def _outproj_kernel(x_ref, ya_ref, yb_ref, wa_ref, wb_ref, o_ref):
    o_ref[...] = x_ref[...] + _dot(ya_ref[...], wa_ref[...]) + _dot(yb_ref[...], wb_ref[...])


def _out_proj(x2d, ya, yb, wa, wb, tm=1024):
    T, D = x2d.shape
    row = lambda i: (i, 0)
    full = lambda i: (0, 0)
    return pl.pallas_call(
        _outproj_kernel, grid=(T // tm,),
        in_specs=[pl.BlockSpec((tm, D), row), pl.BlockSpec((tm, ya.shape[1]), row),
                  pl.BlockSpec((tm, yb.shape[1]), row), pl.BlockSpec(wa.shape, full),
                  pl.BlockSpec(wb.shape, full)],
        out_specs=pl.BlockSpec((tm, D), row),
        out_shape=jax.ShapeDtypeStruct((T, D), F32),
        compiler_params=_cparams("parallel"), name="out_proj",
    )(x2d, ya, yb, wa, wb)


def _ffn_kernel(x_ref, nw_ref, wg_ref, wu_ref, cw_ref, cb_ref, wo_ref, fw_ref, o_ref, h_ref,
                *, final_norm):
    j = pl.program_id(1)
    S = x_ref.shape[1]

    @pl.when(j == 0)
    def _():
        x = x_ref[0]
        h = x * lax.rsqrt(jnp.mean(x * x, axis=-1, keepdims=True) + NORM_EPS) * nw_ref[...]
        h_ref[...] = h.astype(BF16)
        o_ref[0] = x

    hb = h_ref[...]
    gate = _dot(hb, wg_ref[...])
    up = _dot(hb, wu_ref[...])
    row = lax.broadcasted_iota(jnp.int32, gate.shape, 0)
    prev, nxt = _shift_rows(gate, row, S)
    cw = cw_ref[...]
    g = prev * cw[0:1] + gate * cw[1:2] + nxt * cw[2:3] + cb_ref[...]
    act = (_silu(g) * up).astype(BF16)
    o_ref[0] += _dot(act, wo_ref[...])

    if final_norm:
        @pl.when(j == pl.num_programs(1) - 1)
        def _():
            y = o_ref[0]
            o_ref[0] = y * lax.rsqrt(jnp.mean(y * y, axis=-1, keepdims=True) + NORM_EPS) * fw_ref[...]


def _conv_ffn(x, norm_w, w_in, conv_w, conv_b, w_out, final_w, final_norm):
    B, S, D = x.shape
    tf = FF_TILE
    nj = D_FF // tf
    wg = w_in[:, :D_FF].astype(BF16)
    wu = w_in[:, D_FF:].astype(BF16)
    wo = w_out.astype(BF16)
    return pl.pallas_call(
        functools.partial(_ffn_kernel, final_norm=final_norm), grid=(B, nj),
        in_specs=[pl.BlockSpec((1, S, D), lambda b, j: (b, 0, 0)),
                  pl.BlockSpec((1, D), lambda b, j: (0, 0)),
                  pl.BlockSpec((D, tf), lambda b, j: (0, j)),
                  pl.BlockSpec((D, tf), lambda b, j: (0, j)),
                  pl.BlockSpec((3, tf), lambda b, j: (0, j)),
                  pl.BlockSpec((1, tf), lambda b, j: (0, j)),
                  pl.BlockSpec((tf, D), lambda b, j: (j, 0)),
                  pl.BlockSpec((1, D), lambda b, j: (0, 0))],
        out_specs=pl.BlockSpec((1, S, D), lambda b, j: (b, 0, 0)),
        out_shape=jax.ShapeDtypeStruct((B, S, D), F32),
        scratch_shapes=[pltpu.VMEM((S, D), BF16)],
        compiler_params=_cparams("parallel", "arbitrary"), name="conv_ffn",
    )(x, norm_w.reshape(1, D), wg, wu, conv_w, conv_b.reshape(1, D_FF), wo, final_w.reshape(1, D))


def _deltanet_kernel(qkv_ref, z_ref, ba_ref, cw_ref, gp_ref, nw_ref, o_ref,
                     q_s, k_s, v_s, pk_s, o_s, u_s, w_s, in_s, st_s):
    S = qkv_ref.shape[1]
    W, E, C, H = DN_WIDTH, DN_HEAD_DIM, DN_CHUNK, DN_HEADS
    n_chunks = S // C
    row = lax.broadcasted_iota(jnp.int32, (S, LANES), 0)
    lane = lax.broadcasted_iota(jnp.int32, (S, LANES), 1)

    cw = cw_ref[...]
    for t in range(3 * W // LANES):
        sl = slice(t * LANES, (t + 1) * LANES)
        x = qkv_ref[0, :, sl].astype(F32)
        prev, nxt = _shift_rows(x, row, S)
        y = _silu(prev * cw[0:1, sl] + x * cw[1:2, sl] + nxt * cw[2:3, sl])
        h = t % H
        hs = slice(h * E, (h + 1) * E)
        if t < H:
            y = y * lax.rsqrt(jnp.sum(y * y, axis=-1, keepdims=True) + 1e-6) * (E ** -0.5)
            q_s[:, hs] = y.astype(BF16)
        elif t < 2 * H:
            y = y * lax.rsqrt(jnp.sum(y * y, axis=-1, keepdims=True) + 1e-6)
            k_s[:, hs] = y.astype(BF16)
        else:
            v_s[:, hs] = y.astype(BF16)

    ba = ba_ref[0]
    beta = 1.0 / (1.0 + jnp.exp(-ba))
    xg = ba + gp_ref[1:2, :]
    g = gp_ref[0:1, :] * (jnp.maximum(xg, 0.0) + jnp.log1p(jnp.exp(-jnp.abs(xg))))
    pos = row & (C - 1)
    pre = g
    suf = g
    sh = 1
    while sh < C:
        pre = pre + jnp.where(pos >= sh, pltpu.roll(pre, sh, axis=0), 0.0)
        suf = suf + jnp.where(pos < C - sh, pltpu.roll(suf, S - sh, axis=0), 0.0)
        sh *= 2
    fwd = lane < 8 + H
    gc = jnp.where(fwd, pre, suf)
    glc = jnp.where(fwd, suf, pre) - g
    pk = jnp.where(lane < 8, beta,
                   jnp.where(lane < 16, gc,
                             jnp.where(lane < 24, pltpu.roll(glc, 8, axis=1),
                                       pltpu.roll(gc + glc, 16, axis=1))))
    pk_s[...] = pk

    st_s[...] = jnp.zeros_like(st_s)
    o_s[...] = jnp.zeros_like(o_s)

    ri = lax.broadcasted_iota(jnp.int32, (C, LANES), 0)
    ci = lax.broadcasted_iota(jnp.int32, (C, LANES), 1)
    lo = ci < C
    eye_hi = (ci == ri + C).astype(F32)
    zeros_b = jnp.zeros((C, E), BF16)

    heads = [slice(h * E, (h + 1) * E) for h in range(H)]
    chains = [(d, h) for d in range(2) for h in range(H)]


    def pre_body(i, carry):
        units = [(cc, d, h) for cc in range(DN_PRE_CHUNKS) for d, h in chains]
        rows, pkc, tr, k_c, v_c, s2 = [], [], [], [], [], []
        for cc in range(DN_PRE_CHUNKS):
            rw = pl.ds(pl.multiple_of((i * DN_PRE_CHUNKS + cc) * C, C), C)
            rows.append(rw)
            pk_c = pk_s[rw, :]
            pkc.append(pk_c)
            tr.append(jnp.transpose(jnp.concatenate([pk_c, pk_c], axis=0)))
            k_c.append([k_s[rw, hs] for hs in heads])
            v_c.append([v_s[rw, hs] for hs in heads])
            s2.append([_dot_nt(jnp.concatenate([q_s[rw, hs], k], axis=0),
                               jnp.concatenate([k, zeros_b], axis=0))
                       for hs, k in zip(heads, k_c[cc])])
        z, b_row, bg_row = [], [], []
        for cc, d, h in units:
            c = d * H + h
            causal = ((ri >= ci) if d == 0 else (ri <= ci)) & lo
            strict = ((ri > ci) if d == 0 else (ri < ci)) & lo
            beta_col = pkc[cc][:, c:c + 1]
            gc_col = pkc[cc][:, 8 + c:9 + c]
            beta_row = tr[cc][c:c + 1, :]
            gc_row = tr[cc][8 + c:9 + c, :]
            dmat = jnp.where(causal, jnp.exp(jnp.where(causal, gc_col - gc_row, 0.0)), 0.0)
            in_s[c, rows[cc], :] = (s2[cc][h][:C] * dmat).astype(BF16)
            z.append(eye_hi - jnp.where(strict, s2[cc][h][C:] * beta_col * dmat, 0.0))
            b_row.append(beta_row)
            bg_row.append(beta_row * jnp.exp(gc_row))
        for it in range(6):
            zb = [x.astype(BF16) for x in z]
            z = [_dot(b, jnp.concatenate([b, zeros_b], axis=0)) + jnp.where(lo, 0.0, x)
                 for b, x in zip(zb, z)]
        us = [_dot((x * br).astype(BF16), jnp.concatenate([zeros_b, v_c[cc][h]], axis=0))
              for (cc, d, h), x, br in zip(units, z, b_row)]
        ws = [_dot((x * br).astype(BF16), jnp.concatenate([zeros_b, k_c[cc][h]], axis=0))
              for (cc, d, h), x, br in zip(units, z, bg_row)]
        for (cc, d, h), u, w in zip(units, us, ws):
            u_s[d, rows[cc], heads[h]] = u.astype(BF16)
            w_s[d, rows[cc], heads[h]] = w.astype(BF16)
        return carry

    lax.fori_loop(0, n_chunks // DN_PRE_CHUNKS, pre_body, 0)

    def rec_body(n, carry):
        rows, ex = [], []
        for d in range(2):
            nd = n if d == 0 else n_chunks - 1 - n
            rows.append(pl.ds(pl.multiple_of(nd * C, C), C))
            ex.append(jnp.exp(pk_s[rows[d], :]))
        st = [st_s[d * H + h] for d, h in chains]
        wq = [_dot(jnp.concatenate([w_s[d, rows[d], heads[h]], q_s[rows[d], heads[h]]], axis=0),
                   s.astype(BF16)) for (d, h), s in zip(chains, st)]
        for (d, h), s, x in zip(chains, st, wq):
            c = d * H + h
            eg_col = ex[d][:, 8 + c:9 + c]
            eglc_col = ex[d][:, 16 + c:17 + c]
            etot_col = ex[d][:, 24 + c:25 + c]
            v_new = u_s[d, rows[d], heads[h]].astype(F32) - x[:C]
            o = eg_col * x[C:] + _dot(in_s[c, rows[d], :],
                                      jnp.concatenate([v_new.astype(BF16), zeros_b], axis=0))
            upd = _dot_tn(k_s[rows[d], heads[h]], (eglc_col * v_new).astype(BF16))
            st_s[c] = jnp.concatenate([etot_col, etot_col], axis=0) * s + upd
            o_s[rows[d], heads[h]] += o
        return carry

    lax.fori_loop(0, n_chunks, rec_body, 0)

    nw = nw_ref[...]
    for h in range(H):
        hs = slice(h * E, (h + 1) * E)
        o = o_s[:, hs]
        o = o * lax.rsqrt(jnp.mean(o * o, axis=-1, keepdims=True) + NORM_EPS) * nw
        o_ref[0, :, hs] = (o * _silu(z_ref[0, :, hs].astype(F32))).astype(o_ref.dtype)


def _deltanet(qkv, z, ba, conv_w, a_log, dt_bias, norm_w):
    B, S, _ = qkv.shape
    W, E = DN_WIDTH, DN_HEAD_DIM
    gp = jnp.zeros((2, LANES), F32)
    gp = gp.at[0, 8:16].set(-jnp.exp(a_log.astype(F32)).reshape(-1))
    gp = gp.at[1, 8:16].set(dt_bias.astype(F32).reshape(-1))
    bspec = lambda n: pl.BlockSpec((1, S, n), lambda b: (b, 0, 0))
    full = lambda shp: pl.BlockSpec(shp, lambda b: (0,) * len(shp))
    return pl.pallas_call(
        _deltanet_kernel, grid=(B,),
        in_specs=[bspec(3 * W), bspec(W), bspec(LANES), full((3, 3 * W)), full((2, LANES)),
                  full((1, E))],
        out_specs=bspec(W),
        out_shape=jax.ShapeDtypeStruct((B, S, W), BF16),
        scratch_shapes=[pltpu.VMEM((S, W), BF16), pltpu.VMEM((S, W), BF16), pltpu.VMEM((S, W), BF16),
                        pltpu.VMEM((S, LANES), F32), pltpu.VMEM((S, W), F32),
                        pltpu.VMEM((2, S, W), BF16), pltpu.VMEM((2, S, W), BF16),
                        pltpu.VMEM((DN_CHAINS, S, LANES), BF16),
                        pltpu.VMEM((DN_CHAINS, E, E), F32)],
        compiler_params=_cparams("parallel"), name="deltanet",
    )(qkv, z, ba, conv_w.astype(F32), gp, norm_w.reshape(1, E).astype(F32))


def _rope_tables(S, E):
    inv = ROPE_THETA ** (-jnp.arange(0, E, 2, dtype=F32) / E)
    ang = jnp.arange(S, dtype=F32)[:, None] * inv[None, :]
    cos, sin = jnp.cos(ang), jnp.sin(ang)
    cos_t = jnp.tile(jnp.concatenate([cos, cos], axis=-1), (1, LANES // E))
    sin_t = jnp.tile(jnp.concatenate([-sin, sin], axis=-1), (1, LANES // E))
    return cos_t, sin_t


def _rope_tile(x, cos_t, sin_t, lane):
    half = DIL_HEAD_DIM // 2
    partner = jnp.where((lane & (DIL_HEAD_DIM - 1)) < half,
                        pltpu.roll(x, LANES - half, axis=1), pltpu.roll(x, half, axis=1))
    return x * cos_t + partner * sin_t


def _dilated_kernel(q_ref, k_ref, v_ref, cos_ref, sin_ref, o_ref,
                    q_s, k_s, v_s, num_s, den_s, max_s, bias_s):
    S = q_ref.shape[1]
    E = DIL_HEAD_DIM
    QB = DIL_QBLK
    lane = lax.broadcasted_iota(jnp.int32, (S, LANES), 1)
    cos_t = cos_ref[...]
    sin_t = sin_ref[...]
    q_s[...] = _rope_tile(q_ref[0].astype(F32), cos_t, sin_t, lane) * (E ** -0.5)
    k_s[...] = _rope_tile(k_ref[0].astype(F32), cos_t, sin_t, lane)
    v_s[...] = v_ref[0].astype(F32)
    head0 = lax.broadcasted_iota(jnp.int32, (QB, LANES), 1) < E

    for p, (window, dil) in enumerate(DIL_PATTERNS):
        half = window // (2 * dil)
        L = S // dil
        nb = L // QB
        KW = min(QB + 2 * half, L)
        relm = (lax.broadcasted_iota(jnp.int32, (QB, KW), 1)
                - lax.broadcasted_iota(jnp.int32, (QB, KW), 0))
        for case, off in enumerate((0, -half, QB - KW)):
            bias_s[case, :, :KW] = jnp.where(jnp.abs(relm + off) <= half, 0.0, NEG_INF)
        ones_b = jnp.ones((KW, LANES), BF16)

        def block(i, carry, dil=dil, half=half, L=L, nb=nb, KW=KW, p=p, ones_b=ones_b):
            qrows, biases, vas, scores = [], [], [], []
            for uu in range(DIL_UNROLL):
                it = i * DIL_UNROLL + uu
                r = it // nb
                n = it % nb
                q0 = n * QB
                k0 = jnp.clip(q0 - half, 0, L - KW)
                if dil == 1:
                    qr = pl.ds(pl.multiple_of(q0, QB), QB)
                    kr = pl.ds(pl.multiple_of(k0, half), KW)
                else:
                    qr = pl.ds(r + dil * q0, QB, stride=dil)
                    kr = pl.ds(r + dil * k0, KW, stride=dil)
                qrows.append(qr)
                case = jnp.where(n == 0, 0, jnp.where(n == nb - 1, 2, 1))
                biases.append(bias_s[case, :, :KW])
                qf = q_s[qr, :]
                kb = k_s[kr, :].astype(BF16)
                vas.append(jnp.concatenate([v_s[kr, :].astype(BF16), ones_b], axis=1))
                for hh in range(2):
                    qh = jnp.where(head0 if hh == 0 else ~head0, qf, 0.0).astype(BF16)
                    scores.append(_dot_nt(qh, kb))
            ms, es = [], []
            for j, s in enumerate(scores):
                s = s + biases[j // 2]
                m = jnp.max(s, axis=-1, keepdims=True)
                ms.append(m)
                es.append(jnp.exp(s - m).astype(BF16))
            oa = [_dot(e, vas[j // 2]) for j, e in enumerate(es)]
            for uu in range(DIL_UNROLL):
                a0, a1 = oa[2 * uu], oa[2 * uu + 1]
                num_s[p, qrows[uu], :] = jnp.where(head0, a0[:, :LANES], a1[:, :LANES])
                den_s[p, qrows[uu], :] = jnp.where(head0, a0[:, LANES:], a1[:, LANES:])
                max_s[p, qrows[uu], :] = jnp.where(head0, ms[2 * uu], ms[2 * uu + 1])
            return carry

        lax.fori_loop(0, dil * nb // DIL_UNROLL, block, 0)

    mx = jnp.maximum(jnp.maximum(max_s[0], max_s[1]), max_s[2])
    num = jnp.zeros((S, LANES), F32)
    den = jnp.zeros((S, LANES), F32)
    for p in range(len(DIL_PATTERNS)):
        w = jnp.exp(max_s[p] - mx)
        num = num + w * num_s[p]
        den = den + w * den_s[p]
    o_ref[0] = (num * (1.0 / den)).astype(o_ref.dtype)


def _dilated_attention(qkv, cos_t, sin_t):
    B, S, _ = qkv.shape
    nt = DIL_WIDTH // LANES
    tile = lambda off: pl.BlockSpec((1, S, LANES), lambda b, t: (b, 0, t + off))
    tab = pl.BlockSpec((S, LANES), lambda b, t: (0, 0))
    return pl.pallas_call(
        _dilated_kernel, grid=(B, nt),
        in_specs=[tile(0), tile(nt), tile(2 * nt), tab, tab],
        out_specs=tile(0),
        out_shape=jax.ShapeDtypeStruct((B, S, DIL_WIDTH), BF16),
        scratch_shapes=[pltpu.VMEM((S, LANES), F32)] * 3
        + [pltpu.VMEM((len(DIL_PATTERNS), S, LANES), F32)] * 3
        + [pltpu.VMEM((3, DIL_QBLK, 2 * LANES), F32)],
        compiler_params=_cparams("parallel", "parallel"), name="dilated_attention",
    )(qkv, qkv, qkv, cos_t, sin_t)


def _retention_kernel(lg_ref, qk_ref, v_ref, g_ref, cos_ref, sin_ref, o_ref,
                      qk_s, d_s, kv_s, o_s):
    S = qk_ref.shape[1]
    H, EK, EV, C = RET_HEADS, RET_KEY_DIM, RET_VAL_DIM, RET_CHUNK
    n_chunks = S // C
    lane = lax.broadcasted_iota(jnp.int32, (S, LANES), 1)
    cos_t = cos_ref[...]
    sin_t = sin_ref[...]
    for t in range(2 * RET_QK // LANES):
        sl = slice(t * LANES, (t + 1) * LANES)
        y = _rope_tile(qk_ref[0, :, sl].astype(F32), cos_t, sin_t, lane)
        qk_s[:, sl] = y * (EK ** -0.5) if t < RET_QK // LANES else y

    rel = (lax.broadcasted_iota(jnp.int32, (C, C), 0)
           - lax.broadcasted_iota(jnp.int32, (C, C), 1)).astype(F32)
    idx = lax.broadcasted_iota(jnp.int32, (C, EK), 0).astype(F32)
    for h in range(H):
        lgf = lg_ref[0, h]
        lgb = lg_ref[1, h]
        d_s[h] = jnp.exp(lgf * jnp.maximum(rel, 0.0) + lgb * jnp.maximum(-rel, 0.0))
        k_dec = jnp.concatenate([jnp.exp(lgf * (C - 1 - idx)), jnp.exp(lgb * idx)], axis=1)
        q_dec = jnp.concatenate([jnp.exp(lgf * (idx + 1)), jnp.exp(lgb * (C - idx))], axis=1)
        qs = slice(h * EK, (h + 1) * EK)
        ks = slice(RET_QK + h * EK, RET_QK + (h + 1) * EK)
        vs = slice(h * EV, (h + 1) * EV)
        for n in range(n_chunks):
            rows = slice(n * C, (n + 1) * C)
            k_c = qk_s[rows, ks]
            k2 = (jnp.concatenate([k_c, k_c], axis=1) * k_dec).astype(BF16)
            kv_s[n] = _dot_tn(k2, v_ref[0, rows, vs])
        fdec = jnp.exp(lgf * C)
        bdec = jnp.exp(lgb * C)
        st = jnp.zeros((EK, EV), F32)
        for n in range(n_chunks):
            inc = kv_s[n, :EK, :]
            kv_s[n, :EK, :] = st
            st = st * fdec + inc
        st = jnp.zeros((EK, EV), F32)
        for n in range(n_chunks - 1, -1, -1):
            inc = kv_s[n, EK:, :]
            kv_s[n, EK:, :] = st
            st = st * bdec + inc
        for n in range(n_chunks):
            rows = slice(n * C, (n + 1) * C)
            q_c = qk_s[rows, qs]
            k_c = qk_s[rows, ks]
            sc = (_dot_nt(q_c.astype(BF16), k_c.astype(BF16)) * d_s[h]).astype(BF16)
            q2 = (jnp.concatenate([q_c, q_c], axis=1) * q_dec).astype(BF16)
            o_s[rows, vs] = _dot(sc, v_ref[0, rows, vs]) + _dot(q2, kv_s[n].astype(BF16))

    for h in range(H):
        vs = slice(h * EV, (h + 1) * EV)
        o = o_s[:, vs]
        o = o * lax.rsqrt(jnp.mean(o * o, axis=-1, keepdims=True) + NORM_EPS)
        o_ref[0, :, vs] = (o * _silu(g_ref[0, :, vs].astype(F32))).astype(o_ref.dtype)


def _retention(qk, v, gate, log_decay, cos_t, sin_t):
    B, S, _ = qk.shape
    H, EK, EV, C = RET_HEADS, RET_KEY_DIM, RET_VAL_DIM, RET_CHUNK
    bspec = lambda n: pl.BlockSpec((1, S, n), lambda b: (b, 0, 0))
    tab = pl.BlockSpec((S, LANES), lambda b: (0, 0))
    return pl.pallas_call(
        _retention_kernel, grid=(B,),
        in_specs=[pl.BlockSpec(memory_space=pltpu.SMEM), bspec(2 * RET_QK), bspec(RET_V),
                  bspec(RET_V), tab, tab],
        out_specs=bspec(RET_V),
        out_shape=jax.ShapeDtypeStruct((B, S, RET_V), BF16),
        scratch_shapes=[pltpu.VMEM((S, 2 * RET_QK), F32), pltpu.VMEM((H, C, C), F32),
                        pltpu.VMEM((S // C, 2 * EK, EV), F32), pltpu.VMEM((S, RET_V), F32)],
        compiler_params=_cparams("parallel"), name="retention",
    )(log_decay.astype(F32), qk, v, gate, cos_t, sin_t)


def _dft_matrix(S):
    n = 2 * S
    k = jnp.arange(S, dtype=jnp.int32)[:, None]
    t = jnp.arange(S, dtype=jnp.int32)[None, :]
    ang = ((k * t) % n).astype(F32) * (2.0 * math.pi / n)
    cosb = jnp.cos(ang)
    sinb = -jnp.sin(ang)
    nyq = jnp.where(t % 2 == 0, 1.0, -1.0).astype(F32)
    sinb = jnp.where(k == 0, nyq, sinb)
    return jnp.stack([cosb, sinb])


def _hy_filter_kernel(z_ref, w1_ref, b1_ref, f1_ref, w2_ref, b2_ref, f2_ref, w3_ref, dl_ref, o_ref):
    dot = functools.partial(jnp.dot, preferred_element_type=F32, precision=HIGHEST)
    z = z_ref[...]
    hid = jnp.sin(f1_ref[...] * (dot(z, w1_ref[...]) + b1_ref[...]))
    hid = jnp.sin(f2_ref[...] * (dot(hid, w2_ref[...]) + b2_ref[...]))
    o_ref[...] = dot(hid, w3_ref[...]) * jnp.exp(-z[:, 0:1] * dl_ref[...])


def _hy_spectrum_kernel(f_ref, h_ref, o_ref):
    o_ref[0] = jnp.dot(f_ref[0], h_ref[...], preferred_element_type=F32, precision=HIGHEST)


def _hyena_spectrum(S, dft, w1, b1, f1, w2, b2, f2, w3):
    t = jnp.linspace(0.0, 1.0, S, dtype=F32)[:, None]
    bands = (HY_EMB - 1) // 2
    wv = 2.0 * math.pi * jnp.arange(S, dtype=F32) / S
    fr = jnp.linspace(1e-4, bands - 1, bands, dtype=F32)
    ang = wv[:, None] * fr[None, :]
    z = jnp.concatenate([t, jnp.cos(ang), -jnp.sin(ang)], axis=-1)
    z = jnp.pad(z, ((0, 0), (0, LANES - HY_EMB)))
    w1p = jnp.pad(w1.astype(F32), ((0, LANES - HY_EMB), (0, 0)))
    deltas = jnp.abs(jnp.linspace(math.log(HY_TARGET) / HY_SLOW, math.log(HY_TARGET) / HY_FAST,
                                  HY_WIDTH, dtype=F32))
    dl = jnp.tile(deltas, 2)[None, :]
    row = lambda a: a.astype(F32).reshape(1, -1)
    filt = pl.pallas_call(
        _hy_filter_kernel, out_shape=jax.ShapeDtypeStruct((S, 2 * HY_WIDTH), F32),
        compiler_params=pltpu.CompilerParams(vmem_limit_bytes=VMEM_LIMIT), name="hyena_filter",
    )(z, w1p, row(b1), row(f1), w2.astype(F32), row(b2), row(f2), w3.astype(F32), dl)
    tf = HY_FREQ_TILE
    return pl.pallas_call(
        _hy_spectrum_kernel, grid=(2, S // tf),
        in_specs=[pl.BlockSpec((1, tf, S), lambda c, f: (c, f, 0)),
                  pl.BlockSpec((S, 2 * HY_WIDTH), lambda c, f: (0, 0))],
        out_specs=pl.BlockSpec((1, tf, 2 * HY_WIDTH), lambda c, f: (c, f, 0)),
        out_shape=jax.ShapeDtypeStruct((2, S, 2 * HY_WIDTH), F32),
        compiler_params=_cparams("parallel", "parallel"), name="hyena_spectrum",
    )(dft, filt)


def _hyena_kernel(x_ref, cw_ref, cb_ref, bias_ref, f_ref, ft_ref, sp_ref, o_ref,
                  u_s, x0_s, acc_s):
    f = pl.program_id(1)
    S = x_ref.shape[1]
    W = HY_WIDTH
    tf = f_ref.shape[1]

    @pl.when(f == 0)
    def _():
        row = lax.broadcasted_iota(jnp.int32, (S, LANES), 0)
        cw = cw_ref[...]
        cb = cb_ref[...]

        def conv(t):
            sl = slice(t * LANES, (t + 1) * LANES)
            x = x_ref[0, :, sl].astype(F32)
            prev, nxt = _shift_rows(x, row, S)
            return prev * cw[0:1, sl] + x * cw[1:2, sl] + nxt * cw[2:3, sl] + cb[:, sl]

        nt = W // LANES
        for t in range(nt):
            sl = slice(t * LANES, (t + 1) * LANES)
            x0_s[:, sl] = conv(t)
            u_s[:, sl] = (conv(2 * nt + t) * conv(nt + t)).astype(BF16)
        acc_s[...] = jnp.zeros_like(acc_s)

    n = 2 * S
    sp_c = sp_ref[0]
    sp_s = sp_ref[1]
    s_re = sp_c[:, :W] + sp_c[:, W:]
    slot0 = (lax.broadcasted_iota(jnp.int32, (tf, W), 0) + f * tf) == 0
    wk = jnp.where(slot0, 1.0 / n, 2.0 / n)
    ca = s_re * wk
    cb2 = jnp.where(slot0, 0.0, (sp_s[:, :W] - sp_s[:, W:]) * wk)
    cd = jnp.where(slot0, (sp_s[:, :W] + sp_s[:, W:]) * wk, ca)

    u = u_s[...]
    u_re = _dot(f_ref[0], u)
    u_im = _dot(f_ref[1], u)
    y_re = (u_re * ca - u_im * cb2).astype(BF16)
    y_im = (u_re * cb2 + u_im * cd).astype(BF16)
    acc_s[...] += _dot(ft_ref[0], y_re) + _dot(ft_ref[1], y_im)

    @pl.when(f == pl.num_programs(1) - 1)
    def _():
        y = acc_s[...] + u_s[...].astype(F32) * bias_ref[...]
        o_ref[0] = (y * x0_s[...]).astype(o_ref.dtype)


def _hyena(hy, conv_w, conv_b, bias, dft_b, dft_tb, spec):
    B, S, _ = hy.shape
    W = HY_WIDTH
    tf = HY_FREQ_TILE
    full = lambda shp: pl.BlockSpec(shp, lambda b, f: (0,) * len(shp))
    return pl.pallas_call(
        _hyena_kernel, grid=(B, S // tf),
        in_specs=[pl.BlockSpec((1, S, 3 * W), lambda b, f: (b, 0, 0)),
                  full((3, 3 * W)), full((1, 3 * W)), full((1, W)),
                  pl.BlockSpec((2, tf, S), lambda b, f: (0, f, 0)),
                  pl.BlockSpec((2, S, tf), lambda b, f: (0, 0, f)),
                  pl.BlockSpec((2, tf, 2 * W), lambda b, f: (0, f, 0))],
        out_specs=pl.BlockSpec((1, S, W), lambda b, f: (b, 0, 0)),
        out_shape=jax.ShapeDtypeStruct((B, S, W), BF16),
        scratch_shapes=[pltpu.VMEM((S, W), BF16), pltpu.VMEM((S, W), F32), pltpu.VMEM((S, W), F32)],
        compiler_params=_cparams("parallel", "arbitrary"), name="hyena",
    )(hy, conv_w.astype(F32), conv_b.astype(F32).reshape(1, -1), bias.astype(F32).reshape(1, -1),
      dft_b, dft_tb, spec)


def kernel(x, norm_mix, norm_ffn, final_norm, ab_w_in, dn_conv_w, dn_a_log, dn_dt_bias, dn_norm_w, ab_w_out, cd_w_in, ret_log_decay, hy_conv_w, hy_conv_b, hy_w1, hy_b1, hy_f1, hy_w2, hy_b2, hy_f2, hy_w3, hy_bias, cd_w_out, ffn_w_in, ffn_conv_w, ffn_conv_b, ffn_w_out):
    B, S, D = x.shape
    T = B * S
    x2 = x.reshape(T, D)
    cos_t, sin_t = _rope_tables(S, DIL_HEAD_DIM)

    w = ab_w_in[0]
    W = DN_WIDTH
    n_gate = 4 * DN_HEADS
    w_ba = jnp.pad(w[:, 4 * W:4 * W + n_gate], ((0, 0), (0, LANES - n_gate)))
    qkv, z, ba, dil = _norm_proj(
        x2, norm_mix[0],
        [w[:, :3 * W].astype(BF16), w[:, 3 * W:4 * W].astype(BF16), w_ba.astype(BF16),
         w[:, 4 * W + n_gate:].astype(BF16)],
        [BF16, BF16, F32, BF16])
    y_a = _deltanet(qkv.reshape(B, S, -1), z.reshape(B, S, -1), ba.reshape(B, S, -1),
                    dn_conv_w[0], dn_a_log[0], dn_dt_bias[0], dn_norm_w[0])
    y_b = _dilated_attention(dil.reshape(B, S, -1), cos_t, sin_t)
    wo = ab_w_out[0].astype(BF16)
    x2 = _out_proj(x2, y_a.reshape(T, -1), y_b.reshape(T, -1), wo[:W], wo[W:])
    x3 = _conv_ffn(x2.reshape(B, S, D), norm_ffn[0], ffn_w_in[0], ffn_conv_w[0], ffn_conv_b[0],
                   ffn_w_out[0], final_norm, False)

    w = cd_w_in[0]
    c0, c1, c2 = 2 * RET_QK, 2 * RET_QK + RET_V, 2 * RET_QK + 2 * RET_V
    qk, v, gate, hy = _norm_proj(
        x3.reshape(T, D), norm_mix[1],
        [w[:, :c0].astype(BF16), w[:, c0:c1].astype(BF16), w[:, c1:c2].astype(BF16),
         w[:, c2:].astype(BF16)],
        [BF16, BF16, BF16, BF16])
    y_c = _retention(qk.reshape(B, S, -1), v.reshape(B, S, -1), gate.reshape(B, S, -1),
                     ret_log_decay[0], cos_t, sin_t)
    dft = _dft_matrix(S)
    spec = _hyena_spectrum(S, dft, hy_w1[0], hy_b1[0], hy_f1[0], hy_w2[0], hy_b2[0], hy_f2[0],
                           hy_w3[0])
    dft_b = dft.astype(BF16)
    y_d = _hyena(hy.reshape(B, S, -1), hy_conv_w[0], hy_conv_b[0], hy_bias[0],
                 dft_b, jnp.swapaxes(dft_b, 1, 2), spec)
    wo = cd_w_out[0].astype(BF16)
    x2 = _out_proj(x3.reshape(T, D), y_c.reshape(T, -1), y_d.reshape(T, -1), wo[:RET_V], wo[RET_V:])
    return _conv_ffn(x2.reshape(B, S, D), norm_ffn[1], ffn_w_in[1], ffn_conv_w[1], ffn_conv_b[1],
                     ffn_w_out[1], final_norm, True)
```

```python
import functools
import math

import numpy as np
import jax
import jax.numpy as jnp
from jax import lax
from jax.experimental import pallas as pl
from jax.experimental.pallas import tpu as pltpu

F32 = jnp.float32
BF16 = jnp.bfloat16
HIGHEST = lax.Precision.HIGHEST

NORM_EPS = 1e-6
ROPE_THETA = 10000.0
NEG_INF = -1e30

D_MODEL = 1024
DN_HEADS = 4
DN_HEAD_DIM = 128
DN_WIDTH = DN_HEADS * DN_HEAD_DIM
DN_CHUNK = 64
DN_CHAINS = 2 * DN_HEADS
DN_PRE_CHUNKS = 4
DIL_HEADS = 8
DIL_HEAD_DIM = 64
DIL_WIDTH = DIL_HEADS * DIL_HEAD_DIM
DIL_PATTERNS = ((128, 1), (512, 4), (2048, 16))
DIL_QBLK = 128
DIL_UNROLL = 8
RET_HEADS = 4
RET_KEY_DIM = 64
RET_VAL_DIM = 128
RET_QK = RET_HEADS * RET_KEY_DIM
RET_V = RET_HEADS * RET_VAL_DIM
RET_CHUNK = 256
HY_WIDTH = 512
HY_EMB = 33
HY_ORDER = 64
HY_TARGET = 1e-2
HY_FAST = 0.3
HY_SLOW = 1.5
HY_FREQ_TILE = 256
D_FF = 2816
PROJ_CHUNK = 256
FF_ROWS = 1024
FF_SPLIT = 2
FF_CHUNK = 256

LANES = 128
VMEM_LIMIT = 56 * 1024 * 1024


def _cparams(*sem):
    return pltpu.CompilerParams(dimension_semantics=sem, vmem_limit_bytes=VMEM_LIMIT)


def _dot(a, b):
    return jnp.dot(a, b, preferred_element_type=F32)


def _dot_nt(a, b):
    return lax.dot_general(a, b, (((1,), (1,)), ((), ())), preferred_element_type=F32)


def _dot_tn(a, b):
    return lax.dot_general(a, b, (((0,), (0,)), ((), ())), preferred_element_type=F32)


def _silu(x):
    return x * (1.0 / (1.0 + jnp.exp(-x)))


def _rms(x, w):
    return x * lax.rsqrt(jnp.mean(x * x, axis=-1, keepdims=True) + NORM_EPS) * w


SUBLANES = 8


def _conv3_rows(x, w0, w1, w2, edge_prev=0.0, edge_next=0.0):
    n = x.shape[0]
    prev = pltpu.roll(x, 1, axis=0)
    nxt = pltpu.roll(x, n - 1, axis=0)
    y = prev * w0 + x * w1 + nxt * w2
    r8 = lax.broadcasted_iota(jnp.int32, (SUBLANES, x.shape[1]), 0)
    head = y[:SUBLANES] + jnp.where(r8 == 0, (edge_prev - prev[:SUBLANES]) * w0, 0.0)
    tail = y[n - SUBLANES:] + jnp.where(r8 == SUBLANES - 1, (edge_next - nxt[n - SUBLANES:]) * w2, 0.0)
    return jnp.concatenate([head, y[SUBLANES:n - SUBLANES], tail], axis=0)


def _proj_kernel(x_ref, xp_ref, xn_ref, nw_ref, *refs, groups, tiles_per_seq):
    i = pl.program_id(0)
    nw = nw_ref[...]
    hb = _rms(x_ref[...], nw).astype(BF16)
    any_conv = any(g[0] for g in groups)
    if any_conv:
        hh = jnp.concatenate([_rms(xp_ref[...], nw), _rms(xn_ref[...], nw)], axis=0).astype(BF16)
        first = (i % tiles_per_seq) == 0
        last = (i % tiles_per_seq) == tiles_per_seq - 1
    n_in = sum(1 + int(g[0]) + int(g[1]) for g in groups)
    in_refs, out_refs = refs[:n_in], refs[n_in:]
    pos = 0
    for (has_conv, has_bias, acts), o_ref in zip(groups, out_refs):
        w_ref = in_refs[pos]
        pos += 1
        if has_conv:
            cw_ref = in_refs[pos]
            pos += 1
            cb_ref = in_refs[pos] if has_bias else None
            pos += 1 if has_bias else 0
        n_cols = w_ref.shape[1]
        for c0 in range(0, n_cols, PROJ_CHUNK):
            cs = slice(c0, min(c0 + PROJ_CHUNK, n_cols))
            w = w_ref[:, cs]
            y = _dot(hb, w)
            if has_conv:
                halo = _dot(hh, w)
                y = _conv3_rows(y, cw_ref[0:1, cs], cw_ref[1:2, cs], cw_ref[2:3, cs],
                                jnp.where(first, 0.0, halo[7:8]), jnp.where(last, 0.0, halo[8:9]))
                if has_bias:
                    y = y + cb_ref[:, cs]
            for t0 in range(cs.start, cs.stop, LANES):
                yt = y[:, t0 - cs.start:t0 - cs.start + LANES]
                act = acts[t0 // LANES]
                if act is not None:
                    yt = _silu(yt)
                    if act[0] == "silu_l2":
                        yt = yt * (lax.rsqrt(jnp.sum(yt * yt, axis=-1, keepdims=True) + 1e-6) * act[1])
                o_ref[:, t0:t0 + LANES] = yt.astype(o_ref.dtype)


def _norm_proj(x2d, seq_len, norm_w, outs, tm=512):
    T, D = x2d.shape
    nblk8 = T // 8
    const = lambda shp: pl.BlockSpec(shp, lambda i: (0, 0))
    in_specs = [pl.BlockSpec((tm, D), lambda i: (i, 0)),
                pl.BlockSpec((8, D), lambda i: (jnp.maximum(i * (tm // 8) - 1, 0), 0)),
                pl.BlockSpec((8, D), lambda i: (jnp.minimum((i + 1) * (tm // 8), nblk8 - 1), 0)),
                const((1, D))]
    args = [x2d, x2d, x2d, norm_w.reshape(1, D)]
    groups, out_specs, out_shape = [], [], []
    for w, dtype, conv, acts in outs:
        n = w.shape[1]
        args.append(w)
        in_specs.append(const(w.shape))
        if conv is not None:
            cw, cb = conv
            args.append(cw.astype(F32))
            in_specs.append(const((3, n)))
            if cb is not None:
                args.append(cb.astype(F32).reshape(1, n))
                in_specs.append(const((1, n)))
        acts = acts if acts is not None else (None,) * (n // LANES)
        groups.append((conv is not None, conv is not None and conv[1] is not None, tuple(acts)))
        out_specs.append(pl.BlockSpec((tm, n), lambda i: (i, 0)))
        out_shape.append(jax.ShapeDtypeStruct((T, n), dtype))
    return pl.pallas_call(
        functools.partial(_proj_kernel, groups=tuple(groups), tiles_per_seq=seq_len // tm),
        grid=(T // tm,), in_specs=in_specs, out_specs=out_specs,
        out_shape=out_shape, compiler_params=_cparams("parallel"), name="norm_proj",
    )(*args)


def _outproj_kernel(x_ref, ya_ref, yb_ref, wa_ref, wb_ref, o_ref):
    o_ref[...] = x_ref[...] + _dot(ya_ref[...], wa_ref[...]) + _dot(yb_ref[...], wb_ref[...])


def _out_proj(x2d, ya, yb, wa, wb, tm=1024):
    T, D = x2d.shape
    row = lambda i: (i, 0)
    full = lambda i: (0, 0)
    return pl.pallas_call(
        _outproj_kernel, grid=(T // tm,),
        in_specs=[pl.BlockSpec((tm, D), row), pl.BlockSpec((tm, ya.shape[1]), row),
                  pl.BlockSpec((tm, yb.shape[1]), row), pl.BlockSpec(wa.shape, full),
                  pl.BlockSpec(wb.shape, full)],
        out_specs=pl.BlockSpec((tm, D), row),
        out_shape=jax.ShapeDtypeStruct((T, D), F32),
        compiler_params=_cparams("parallel"), name="out_proj",
    )(x2d, ya, yb, wa, wb)


def _ffn_kernel(x_ref, xp_ref, xn_ref, nw_ref, wg_ref, wu_ref, cw_ref, cb_ref, wo_ref, fw_ref,
                o_ref, h_s, halo_s, act_s, *, final_norm, tiles_per_seq):
    i = pl.program_id(0)
    j = pl.program_id(1)
    tm = x_ref.shape[0]
    tfo = wg_ref.shape[1]

    @pl.when(j == 0)
    def _():
        nw = nw_ref[...]
        h_s[...] = _rms(x_ref[...], nw).astype(BF16)
        halo_s[0:8, :] = _rms(xp_ref[...], nw).astype(BF16)
        halo_s[8:16, :] = _rms(xn_ref[...], nw).astype(BF16)

    first = (i % tiles_per_seq) == 0
    last = (i % tiles_per_seq) == tiles_per_seq - 1
    hb = h_s[...]
    hh = halo_s[...]
    for c0 in range(0, tfo, FF_CHUNK):
        cs = slice(c0, min(c0 + FF_CHUNK, tfo))
        wg = wg_ref[:, cs]
        gate = _dot(hb, wg)
        up = _dot(hb, wu_ref[:, cs])
        gh = _dot(hh, wg)
        g_prev = jnp.where(first, 0.0, gh[7:8])
        g_next = jnp.where(last, 0.0, gh[8:9])
        g = _conv3_rows(gate, cw_ref[0:1, cs], cw_ref[1:2, cs], cw_ref[2:3, cs],
                        g_prev, g_next) + cb_ref[:, cs]
        act_s[j, :, cs] = (_silu(g) * up).astype(BF16)

    @pl.when(j == pl.num_programs(1) - 1)
    def _():
        y = x_ref[...]
        for jj in range(act_s.shape[0]):
            y = y + _dot(act_s[jj], wo_ref[jj * tfo:(jj + 1) * tfo, :])
        o_ref[...] = _rms(y, fw_ref[...]) if final_norm else y


def _conv_ffn(x2d, seq_len, norm_w, w_in, conv_w, conv_b, w_out, final_w, final_norm):
    T, D = x2d.shape
    tm = FF_ROWS
    nj = FF_SPLIT
    tfo = D_FF // nj
    nblk8 = T // 8
    wg = w_in[:, :D_FF].astype(BF16)
    wu = w_in[:, D_FF:].astype(BF16)
    wo = w_out.astype(BF16)
    return pl.pallas_call(
        functools.partial(_ffn_kernel, final_norm=final_norm, tiles_per_seq=seq_len // tm),
        grid=(T // tm, nj),
        in_specs=[pl.BlockSpec((tm, D), lambda i, j: (i, 0)),
                  pl.BlockSpec((8, D), lambda i, j: (jnp.maximum(i * (tm // 8) - 1, 0), 0)),
                  pl.BlockSpec((8, D), lambda i, j: (jnp.minimum((i + 1) * (tm // 8), nblk8 - 1), 0)),
                  pl.BlockSpec((1, D), lambda i, j: (0, 0)),
                  pl.BlockSpec((D, tfo), lambda i, j: (0, j)),
                  pl.BlockSpec((D, tfo), lambda i, j: (0, j)),
                  pl.BlockSpec((3, tfo), lambda i, j: (0, j)),
                  pl.BlockSpec((1, tfo), lambda i, j: (0, j)),
                  pl.BlockSpec((D_FF, D), lambda i, j: (0, 0)),
                  pl.BlockSpec((1, D), lambda i, j: (0, 0))],
        out_specs=pl.BlockSpec((tm, D), lambda i, j: (i, 0)),
        out_shape=jax.ShapeDtypeStruct((T, D), F32),
        scratch_shapes=[pltpu.VMEM((tm, D), BF16), pltpu.VMEM((16, D), BF16),
                        pltpu.VMEM((nj, tm, tfo), BF16)],
        compiler_params=_cparams("parallel", "arbitrary"), name="conv_ffn",
    )(x2d, x2d, x2d, norm_w.reshape(1, D), wg, wu, conv_w, conv_b.reshape(1, D_FF), wo,
      final_w.reshape(1, D))


def _deltanet_kernel(q_ref, k_ref, v_ref, zs_ref, ba_ref, gp_ref, nw_ref, o_ref,
                     pk_s, o_s, u_s, w_s, in_s, st_s):
    S = q_ref.shape[1]
    W, E, C, H = DN_WIDTH, DN_HEAD_DIM, DN_CHUNK, DN_HEADS
    n_chunks = S // C
    row = lax.broadcasted_iota(jnp.int32, (S, LANES), 0)
    lane = lax.broadcasted_iota(jnp.int32, (S, LANES), 1)
    q_s, k_s, v_s = q_ref.at[0], k_ref.at[0], v_ref.at[0]

    ba = ba_ref[0]
    beta = 1.0 / (1.0 + jnp.exp(-ba))
    xg = ba + gp_ref[1:2, :]
    g = gp_ref[0:1, :] * (jnp.maximum(xg, 0.0) + jnp.log1p(jnp.exp(-jnp.abs(xg))))
    pos = row & (C - 1)
    pre = g
    suf = g
    sh = 1
    while sh < C:
        pre = pre + jnp.where(pos >= sh, pltpu.roll(pre, sh, axis=0), 0.0)
        suf = suf + jnp.where(pos < C - sh, pltpu.roll(suf, S - sh, axis=0), 0.0)
        sh *= 2
    fwd = lane < 8 + H
    gc = jnp.where(fwd, pre, suf)
    glc = jnp.where(fwd, suf, pre) - g
    pk = jnp.where(lane < 8, beta,
                   jnp.where(lane < 16, gc,
                             jnp.where(lane < 24, pltpu.roll(glc, 8, axis=1),
                                       pltpu.roll(gc + glc, 16, axis=1))))
    pk_s[...] = pk

    st_s[...] = jnp.zeros_like(st_s)
    o_s[...] = jnp.zeros_like(o_s)

    ri = lax.broadcasted_iota(jnp.int32, (C, LANES), 0)
    ci = lax.broadcasted_iota(jnp.int32, (C, LANES), 1)
    lo = ci < C
    eye_hi = (ci == ri + C).astype(F32)
    zeros_b = jnp.zeros((C, E), BF16)

    heads = [slice(h * E, (h + 1) * E) for h in range(H)]
    chains = [(d, h) for d in range(2) for h in range(H)]


    def pre_body(i, carry):
        units = [(cc, d, h) for cc in range(DN_PRE_CHUNKS) for d, h in chains]
        rows, pkc, tr, k_c, v_c, s2 = [], [], [], [], [], []
        for cc in range(DN_PRE_CHUNKS):
            rw = pl.ds(pl.multiple_of((i * DN_PRE_CHUNKS + cc) * C, C), C)
            rows.append(rw)
            pk_c = pk_s[rw, :]
            pkc.append(pk_c)
            tr.append(jnp.transpose(jnp.concatenate([pk_c, pk_c], axis=0)))
            k_c.append([k_s[rw, hs] for hs in heads])
            v_c.append([v_s[rw, hs] for hs in heads])
            s2.append([_dot_nt(jnp.concatenate([q_s[rw, hs], k], axis=0),
                               jnp.concatenate([k, zeros_b], axis=0))
                       for hs, k in zip(heads, k_c[cc])])
        z, b_row, bg_row = [], [], []
        for cc, d, h in units:
            c = d * H + h
            causal = ((ri >= ci) if d == 0 else (ri <= ci)) & lo
            strict = ((ri > ci) if d == 0 else (ri < ci)) & lo
            beta_col = pkc[cc][:, c:c + 1]
            gc_col = pkc[cc][:, 8 + c:9 + c]
            beta_row = tr[cc][c:c + 1, :]
            gc_row = tr[cc][8 + c:9 + c, :]
            dmat = jnp.where(causal, jnp.exp(jnp.where(causal, gc_col - gc_row, 0.0)), 0.0)
            in_s[c, rows[cc], :] = (s2[cc][h][:C] * dmat).astype(BF16)
            z.append(eye_hi - jnp.where(strict, s2[cc][h][C:] * beta_col * dmat, 0.0))
            b_row.append(beta_row)
            bg_row.append(beta_row * jnp.exp(gc_row))
        for it in range(6):
            zb = [x.astype(BF16) for x in z]
            z = [_dot(b, jnp.concatenate([b, zeros_b], axis=0)) + jnp.where(lo, 0.0, x)
                 for b, x in zip(zb, z)]
        us = [_dot((x * br).astype(BF16), jnp.concatenate([zeros_b, v_c[cc][h]], axis=0))
              for (cc, d, h), x, br in zip(units, z, b_row)]
        ws = [_dot((x * br).astype(BF16), jnp.concatenate([zeros_b, k_c[cc][h]], axis=0))
              for (cc, d, h), x, br in zip(units, z, bg_row)]
        for (cc, d, h), u, w in zip(units, us, ws):
            u_s[d, rows[cc], heads[h]] = u.astype(BF16)
            w_s[d, rows[cc], heads[h]] = w.astype(BF16)
        return carry

    lax.fori_loop(0, n_chunks // DN_PRE_CHUNKS, pre_body, 0)

    def rec_body(n, carry):
        rows, ex = [], []
        for d in range(2):
            nd = n if d == 0 else n_chunks - 1 - n
            rows.append(pl.ds(pl.multiple_of(nd * C, C), C))
            ex.append(jnp.exp(pk_s[rows[d], :]))
        st = [st_s[d * H + h] for d, h in chains]
        wq = [_dot(jnp.concatenate([w_s[d, rows[d], heads[h]], q_s[rows[d], heads[h]]], axis=0),
                   s.astype(BF16)) for (d, h), s in zip(chains, st)]
        for (d, h), s, x in zip(chains, st, wq):
            c = d * H + h
            eg_col = ex[d][:, 8 + c:9 + c]
            eglc_col = ex[d][:, 16 + c:17 + c]
            etot_col = ex[d][:, 24 + c:25 + c]
            v_new = u_s[d, rows[d], heads[h]].astype(F32) - x[:C]
            o = eg_col * x[C:] + _dot(in_s[c, rows[d], :],
                                      jnp.concatenate([v_new.astype(BF16), zeros_b], axis=0))
            upd = _dot_tn(k_s[rows[d], heads[h]], (eglc_col * v_new).astype(BF16))
            st_s[c] = jnp.concatenate([etot_col, etot_col], axis=0) * s + upd
            o_s[rows[d], heads[h]] += o
        return carry

    lax.fori_loop(0, n_chunks, rec_body, 0)

    nw = nw_ref[...]
    for h in range(H):
        hs = slice(h * E, (h + 1) * E)
        o = o_s[:, hs]
        o = o * lax.rsqrt(jnp.mean(o * o, axis=-1, keepdims=True) + NORM_EPS) * nw
        o_ref[0, :, hs] = (o * zs_ref[0, :, hs].astype(F32)).astype(o_ref.dtype)


def _deltanet(qkv, zs, ba, a_log, dt_bias, norm_w):
    B, S, _ = qkv.shape
    W, E = DN_WIDTH, DN_HEAD_DIM
    gp = jnp.zeros((2, LANES), F32)
    gp = gp.at[0, 8:16].set(-jnp.exp(a_log.astype(F32)).reshape(-1))
    gp = gp.at[1, 8:16].set(dt_bias.astype(F32).reshape(-1))
    bspec = lambda n: pl.BlockSpec((1, S, n), lambda b: (b, 0, 0))
    full = lambda shp: pl.BlockSpec(shp, lambda b: (0,) * len(shp))
    return pl.pallas_call(
        _deltanet_kernel, grid=(B,),
        in_specs=[pl.BlockSpec((1, S, W), lambda b: (b, 0, 0)),
                  pl.BlockSpec((1, S, W), lambda b: (b, 0, 1)),
                  pl.BlockSpec((1, S, W), lambda b: (b, 0, 2)),
                  bspec(W), bspec(LANES), full((2, LANES)), full((1, E))],
        out_specs=bspec(W),
        out_shape=jax.ShapeDtypeStruct((B, S, W), BF16),
        scratch_shapes=[pltpu.VMEM((S, LANES), F32), pltpu.VMEM((S, W), F32),
                        pltpu.VMEM((2, S, W), BF16), pltpu.VMEM((2, S, W), BF16),
                        pltpu.VMEM((DN_CHAINS, S, LANES), BF16),
                        pltpu.VMEM((DN_CHAINS, E, E), F32)],
        compiler_params=_cparams("parallel"), name="deltanet",
    )(qkv, qkv, qkv, zs, ba, gp, norm_w.reshape(1, E).astype(F32))


def _rope_tables(S, E):
    inv = ROPE_THETA ** (-jnp.arange(0, E, 2, dtype=F32) / E)
    ang = jnp.arange(S, dtype=F32)[:, None] * inv[None, :]
    cos, sin = jnp.cos(ang), jnp.sin(ang)
    cos_t = jnp.tile(jnp.concatenate([cos, cos], axis=-1), (1, LANES // E))
    sin_t = jnp.tile(jnp.concatenate([-sin, sin], axis=-1), (1, LANES // E))
    return cos_t, sin_t


def _rope_tile(x, cos_t, sin_t, lane):
    half = DIL_HEAD_DIM // 2
    partner = jnp.where((lane & (DIL_HEAD_DIM - 1)) < half,
                        pltpu.roll(x, LANES - half, axis=1), pltpu.roll(x, half, axis=1))
    return x * cos_t + partner * sin_t


def _dilated_kernel(q_ref, k_ref, v_ref, cos_ref, sin_ref, o_ref,
                    q_s, k_s, v_s, num_s, den_s, max_s, bias_s):
    S = q_ref.shape[1]
    E = DIL_HEAD_DIM
    QB = DIL_QBLK
    lane = lax.broadcasted_iota(jnp.int32, (S, LANES), 1)
    cos_t = cos_ref[...]
    sin_t = sin_ref[...]
    q_s[...] = _rope_tile(q_ref[0].astype(F32), cos_t, sin_t, lane) * (E ** -0.5)
    k_s[...] = _rope_tile(k_ref[0].astype(F32), cos_t, sin_t, lane)
    v_s[...] = v_ref[0].astype(F32)
    head0 = lax.broadcasted_iota(jnp.int32, (QB, LANES), 1) < E

    for p, (window, dil) in enumerate(DIL_PATTERNS):
        half = window // (2 * dil)
        L = S // dil
        nb = L // QB
        KW = min(QB + 2 * half, L)
        relm = (lax.broadcasted_iota(jnp.int32, (QB, KW), 1)
                - lax.broadcasted_iota(jnp.int32, (QB, KW), 0))
        for case, off in enumerate((0, -half, QB - KW)):
            bias_s[case, :, :KW] = jnp.where(jnp.abs(relm + off) <= half, 0.0, NEG_INF)
        ones_b = jnp.ones((KW, LANES), BF16)

        def block(i, carry, dil=dil, half=half, L=L, nb=nb, KW=KW, p=p, ones_b=ones_b):
            qrows, biases, vas, scores = [], [], [], []
            for uu in range(DIL_UNROLL):
                it = i * DIL_UNROLL + uu
                r = it // nb
                n = it % nb
                q0 = n * QB
                k0 = jnp.clip(q0 - half, 0, L - KW)
                if dil == 1:
                    qr = pl.ds(pl.multiple_of(q0, QB), QB)
                    kr = pl.ds(pl.multiple_of(k0, half), KW)
                else:
                    qr = pl.ds(r + dil * q0, QB, stride=dil)
                    kr = pl.ds(r + dil * k0, KW, stride=dil)
                qrows.append(qr)
                case = jnp.where(n == 0, 0, jnp.where(n == nb - 1, 2, 1))
                biases.append(bias_s[case, :, :KW])
                qf = q_s[qr, :]
                kb = k_s[kr, :].astype(BF16)
                vas.append(jnp.concatenate([v_s[kr, :].astype(BF16), ones_b], axis=1))
                for hh in range(2):
                    qh = jnp.where(head0 if hh == 0 else ~head0, qf, 0.0).astype(BF16)
                    scores.append(_dot_nt(qh, kb))
            ms, es = [], []
            for j, s in enumerate(scores):
                s = s + biases[j // 2]
                m = jnp.max(s, axis=-1, keepdims=True)
                ms.append(m)
                es.append(jnp.exp(s - m).astype(BF16))
            oa = [_dot(e, vas[j // 2]) for j, e in enumerate(es)]
            for uu in range(DIL_UNROLL):
                a0, a1 = oa[2 * uu], oa[2 * uu + 1]
                num_s[p, qrows[uu], :] = jnp.where(head0, a0[:, :LANES], a1[:, :LANES])
                den_s[p, qrows[uu], :] = jnp.where(head0, a0[:, LANES:], a1[:, LANES:])
                max_s[p, qrows[uu], :] = jnp.where(head0, ms[2 * uu], ms[2 * uu + 1])
            return carry

        lax.fori_loop(0, dil * nb // DIL_UNROLL, block, 0)

    mx = jnp.maximum(jnp.maximum(max_s[0], max_s[1]), max_s[2])
    num = jnp.zeros((S, LANES), F32)
    den = jnp.zeros((S, LANES), F32)
    for p in range(len(DIL_PATTERNS)):
        w = jnp.exp(max_s[p] - mx)
        num = num + w * num_s[p]
        den = den + w * den_s[p]
    o_ref[0] = (num * (1.0 / den)).astype(o_ref.dtype)


def _dilated_attention(qkv, cos_t, sin_t):
    B, S, _ = qkv.shape
    nt = DIL_WIDTH // LANES
    tile = lambda off: pl.BlockSpec((1, S, LANES), lambda b, t: (b, 0, t + off))
    tab = pl.BlockSpec((S, LANES), lambda b, t: (0, 0))
    return pl.pallas_call(
        _dilated_kernel, grid=(B, nt),
        in_specs=[tile(0), tile(nt), tile(2 * nt), tab, tab],
        out_specs=tile(0),
        out_shape=jax.ShapeDtypeStruct((B, S, DIL_WIDTH), BF16),
        scratch_shapes=[pltpu.VMEM((S, LANES), F32)] * 3
        + [pltpu.VMEM((len(DIL_PATTERNS), S, LANES), F32)] * 3
        + [pltpu.VMEM((3, DIL_QBLK, 2 * LANES), F32)],
        compiler_params=_cparams("parallel", "parallel"), name="dilated_attention",
    )(qkv, qkv, qkv, cos_t, sin_t)


def _retention_kernel(lg_ref, qk_ref, v_ref, g_ref, cos_ref, sin_ref, o_ref,
                      qk_s, d_s, kv_s, o_s):
    S = qk_ref.shape[1]
    H, EK, EV, C = RET_HEADS, RET_KEY_DIM, RET_VAL_DIM, RET_CHUNK
    n_chunks = S // C
    lane = lax.broadcasted_iota(jnp.int32, (S, LANES), 1)
    cos_t = cos_ref[...]
    sin_t = sin_ref[...]
    for t in range(2 * RET_QK // LANES):
        sl = slice(t * LANES, (t + 1) * LANES)
        y = _rope_tile(qk_ref[0, :, sl].astype(F32), cos_t, sin_t, lane)
        qk_s[:, sl] = y * (EK ** -0.5) if t < RET_QK // LANES else y

    rel = (lax.broadcasted_iota(jnp.int32, (C, C), 0)
           - lax.broadcasted_iota(jnp.int32, (C, C), 1)).astype(F32)
    idx = lax.broadcasted_iota(jnp.int32, (C, EK), 0).astype(F32)
    for h in range(H):
        lgf = lg_ref[0, h]
        lgb = lg_ref[1, h]
        d_s[h] = jnp.exp(lgf * jnp.maximum(rel, 0.0) + lgb * jnp.maximum(-rel, 0.0))
        k_dec = jnp.concatenate([jnp.exp(lgf * (C - 1 - idx)), jnp.exp(lgb * idx)], axis=1)
        q_dec = jnp.concatenate([jnp.exp(lgf * (idx + 1)), jnp.exp(lgb * (C - idx))], axis=1)
        qs = slice(h * EK, (h + 1) * EK)
        ks = slice(RET_QK + h * EK, RET_QK + (h + 1) * EK)
        vs = slice(h * EV, (h + 1) * EV)
        for n in range(n_chunks):
            rows = slice(n * C, (n + 1) * C)
            k_c = qk_s[rows, ks]
            k2 = (jnp.concatenate([k_c, k_c], axis=1) * k_dec).astype(BF16)
            kv_s[n] = _dot_tn(k2, v_ref[0, rows, vs])
        fdec = jnp.exp(lgf * C)
        bdec = jnp.exp(lgb * C)
        st = jnp.zeros((EK, EV), F32)
        for n in range(n_chunks):
            inc = kv_s[n, :EK, :]
            kv_s[n, :EK, :] = st
            st = st * fdec + inc
        st = jnp.zeros((EK, EV), F32)
        for n in range(n_chunks - 1, -1, -1):
            inc = kv_s[n, EK:, :]
            kv_s[n, EK:, :] = st
            st = st * bdec + inc
        for n in range(n_chunks):
            rows = slice(n * C, (n + 1) * C)
            q_c = qk_s[rows, qs]
            k_c = qk_s[rows, ks]
            sc = (_dot_nt(q_c.astype(BF16), k_c.astype(BF16)) * d_s[h]).astype(BF16)
            q2 = (jnp.concatenate([q_c, q_c], axis=1) * q_dec).astype(BF16)
            o_s[rows, vs] = _dot(sc, v_ref[0, rows, vs]) + _dot(q2, kv_s[n].astype(BF16))

    for h in range(H):
        vs = slice(h * EV, (h + 1) * EV)
        o = o_s[:, vs]
        o = o * lax.rsqrt(jnp.mean(o * o, axis=-1, keepdims=True) + NORM_EPS)
        o_ref[0, :, vs] = (o * g_ref[0, :, vs].astype(F32)).astype(o_ref.dtype)


def _retention(qk, v, gate, log_decay, cos_t, sin_t):
    B, S, _ = qk.shape
    H, EK, EV, C = RET_HEADS, RET_KEY_DIM, RET_VAL_DIM, RET_CHUNK
    bspec = lambda n: pl.BlockSpec((1, S, n), lambda b: (b, 0, 0))
    tab = pl.BlockSpec((S, LANES), lambda b: (0, 0))
    return pl.pallas_call(
        _retention_kernel, grid=(B,),
        in_specs=[pl.BlockSpec(memory_space=pltpu.SMEM), bspec(2 * RET_QK), bspec(RET_V),
                  bspec(RET_V), tab, tab],
        out_specs=bspec(RET_V),
        out_shape=jax.ShapeDtypeStruct((B, S, RET_V), BF16),
        scratch_shapes=[pltpu.VMEM((S, 2 * RET_QK), F32), pltpu.VMEM((H, C, C), F32),
                        pltpu.VMEM((S // C, 2 * EK, EV), F32), pltpu.VMEM((S, RET_V), F32)],
        compiler_params=_cparams("parallel"), name="retention",
    )(log_decay.astype(F32), qk, v, gate, cos_t, sin_t)


def _dft_matrix(S):
    n = 2 * S
    k = jnp.arange(S, dtype=jnp.int32)[:, None]
    t = jnp.arange(S, dtype=jnp.int32)[None, :]
    ang = ((k * t) % n).astype(F32) * (2.0 * math.pi / n)
    cosb = jnp.cos(ang)
    sinb = -jnp.sin(ang)
    nyq = jnp.where(t % 2 == 0, 1.0, -1.0).astype(F32)
    sinb = jnp.where(k == 0, nyq, sinb)
    return jnp.stack([cosb, sinb])


def _hy_filter_kernel(z_ref, w1_ref, b1_ref, f1_ref, w2_ref, b2_ref, f2_ref, w3_ref, dl_ref, o_ref):
    dot = functools.partial(jnp.dot, preferred_element_type=F32, precision=HIGHEST)
    z = z_ref[...]
    hid = jnp.sin(f1_ref[...] * (dot(z, w1_ref[...]) + b1_ref[...]))
    hid = jnp.sin(f2_ref[...] * (dot(hid, w2_ref[...]) + b2_ref[...]))
    o_ref[...] = dot(hid, w3_ref[...]) * jnp.exp(-z[:, 0:1] * dl_ref[...])


def _hy_spectrum_kernel(f_ref, h_ref, o_ref):
    o_ref[0] = jnp.dot(f_ref[0], h_ref[...], preferred_element_type=F32, precision=HIGHEST)


def _hyena_spectrum(S, dft, w1, b1, f1, w2, b2, f2, w3):
    t = jnp.linspace(0.0, 1.0, S, dtype=F32)[:, None]
    bands = (HY_EMB - 1) // 2
    wv = 2.0 * math.pi * jnp.arange(S, dtype=F32) / S
    fr = jnp.linspace(1e-4, bands - 1, bands, dtype=F32)
    ang = wv[:, None] * fr[None, :]
    z = jnp.concatenate([t, jnp.cos(ang), -jnp.sin(ang)], axis=-1)
    z = jnp.pad(z, ((0, 0), (0, LANES - HY_EMB)))
    w1p = jnp.pad(w1.astype(F32), ((0, LANES - HY_EMB), (0, 0)))
    deltas = jnp.abs(jnp.linspace(math.log(HY_TARGET) / HY_SLOW, math.log(HY_TARGET) / HY_FAST,
                                  HY_WIDTH, dtype=F32))
    dl = jnp.tile(deltas, 2)[None, :]
    row = lambda a: a.astype(F32).reshape(1, -1)
    filt = pl.pallas_call(
        _hy_filter_kernel, out_shape=jax.ShapeDtypeStruct((S, 2 * HY_WIDTH), F32),
        compiler_params=pltpu.CompilerParams(vmem_limit_bytes=VMEM_LIMIT), name="hyena_filter",
    )(z, w1p, row(b1), row(f1), w2.astype(F32), row(b2), row(f2), w3.astype(F32), dl)
    tf = HY_FREQ_TILE
    return pl.pallas_call(
        _hy_spectrum_kernel, grid=(2, S // tf),
        in_specs=[pl.BlockSpec((1, tf, S), lambda c, f: (c, f, 0)),
                  pl.BlockSpec((S, 2 * HY_WIDTH), lambda c, f: (0, 0))],
        out_specs=pl.BlockSpec((1, tf, 2 * HY_WIDTH), lambda c, f: (c, f, 0)),
        out_shape=jax.ShapeDtypeStruct((2, S, 2 * HY_WIDTH), F32),
        compiler_params=_cparams("parallel", "parallel"), name="hyena_spectrum",
    )(dft, filt)


def _hyena_kernel(x0_ref, x1_ref, v_ref, bias_ref, f_ref, ft_ref, sp_ref, o_ref, u_s, acc_s):
    f = pl.program_id(1)
    S = v_ref.shape[1]
    W = HY_WIDTH
    tf = f_ref.shape[1]

    @pl.when(f == 0)
    def _():
        u_s[...] = (v_ref[0].astype(F32) * x1_ref[0].astype(F32)).astype(BF16)
        acc_s[...] = jnp.zeros_like(acc_s)

    n = 2 * S
    sp_c = sp_ref[0]
    sp_s = sp_ref[1]
    s_re = sp_c[:, :W] + sp_c[:, W:]
    slot0 = (lax.broadcasted_iota(jnp.int32, (tf, W), 0) + f * tf) == 0
    wk = jnp.where(slot0, 1.0 / n, 2.0 / n)
    ca = s_re * wk
    cb2 = jnp.where(slot0, 0.0, (sp_s[:, :W] - sp_s[:, W:]) * wk)
    cd = jnp.where(slot0, (sp_s[:, :W] + sp_s[:, W:]) * wk, ca)

    u = u_s[...]
    u_re = _dot(f_ref[0], u)
    u_im = _dot(f_ref[1], u)
    y_re = (u_re * ca - u_im * cb2).astype(BF16)
    y_im = (u_re * cb2 + u_im * cd).astype(BF16)
    acc_s[...] += _dot(ft_ref[0], y_re) + _dot(ft_ref[1], y_im)

    @pl.when(f == pl.num_programs(1) - 1)
    def _():
        y = acc_s[...] + u_s[...].astype(F32) * bias_ref[...]
        o_ref[0] = (y * x0_ref[0].astype(F32)).astype(o_ref.dtype)


def _hyena(uc, bias, dft_b, dft_tb, spec):
    B, S, _ = uc.shape
    W = HY_WIDTH
    tf = HY_FREQ_TILE
    full = lambda shp: pl.BlockSpec(shp, lambda b, f: (0,) * len(shp))
    slab = lambda c: pl.BlockSpec((1, S, W), lambda b, f: (b, 0, c))
    return pl.pallas_call(
        _hyena_kernel, grid=(B, S // tf),
        in_specs=[slab(0), slab(1), slab(2), full((1, W)),
                  pl.BlockSpec((2, tf, S), lambda b, f: (0, f, 0)),
                  pl.BlockSpec((2, S, tf), lambda b, f: (0, 0, f)),
                  pl.BlockSpec((2, tf, 2 * W), lambda b, f: (0, f, 0))],
        out_specs=pl.BlockSpec((1, S, W), lambda b, f: (b, 0, 0)),
        out_shape=jax.ShapeDtypeStruct((B, S, W), BF16),
        scratch_shapes=[pltpu.VMEM((S, W), BF16), pltpu.VMEM((S, W), F32)],
        compiler_params=_cparams("parallel", "arbitrary"), name="hyena",
    )(uc, uc, uc, bias.astype(F32).reshape(1, -1), dft_b, dft_tb, spec)


def kernel(x, norm_mix, norm_ffn, final_norm, ab_w_in, dn_conv_w, dn_a_log, dn_dt_bias, dn_norm_w, ab_w_out, cd_w_in, ret_log_decay, hy_conv_w, hy_conv_b, hy_w1, hy_b1, hy_f1, hy_w2, hy_b2, hy_f2, hy_w3, hy_bias, cd_w_out, ffn_w_in, ffn_conv_w, ffn_conv_b, ffn_w_out):
    B, S, D = x.shape
    T = B * S
    x2 = x.reshape(T, D)
    cos_t, sin_t = _rope_tables(S, DIL_HEAD_DIM)

    w = ab_w_in[0]
    W = DN_WIDTH
    n_gate = 4 * DN_HEADS
    w_ba = jnp.pad(w[:, 4 * W:4 * W + n_gate], ((0, 0), (0, LANES - n_gate)))
    nh = DN_WIDTH // LANES
    qkv_acts = ((("silu_l2", DN_HEAD_DIM ** -0.5),) * nh + (("silu_l2", 1.0),) * nh
                + (("silu", 1.0),) * nh)
    qkv, zs, ba, dil = _norm_proj(
        x2, S, norm_mix[0],
        [(w[:, :3 * W].astype(BF16), BF16, (dn_conv_w[0], None), qkv_acts),
         (w[:, 3 * W:4 * W].astype(BF16), BF16, None, (("silu", 1.0),) * nh),
         (w_ba.astype(BF16), F32, None, None),
         (w[:, 4 * W + n_gate:].astype(BF16), BF16, None, None)])
    y_a = _deltanet(qkv.reshape(B, S, -1), zs.reshape(B, S, -1), ba.reshape(B, S, -1),
                    dn_a_log[0], dn_dt_bias[0], dn_norm_w[0])
    y_b = _dilated_attention(dil.reshape(B, S, -1), cos_t, sin_t)
    wo = ab_w_out[0].astype(BF16)
    x2 = _out_proj(x2, y_a.reshape(T, -1), y_b.reshape(T, -1), wo[:W], wo[W:])
    x3 = _conv_ffn(x2, S, norm_ffn[0], ffn_w_in[0], ffn_conv_w[0], ffn_conv_b[0],
                   ffn_w_out[0], final_norm, False)

    w = cd_w_in[0]
    c0, c1, c2 = 2 * RET_QK, 2 * RET_QK + RET_V, 2 * RET_QK + 2 * RET_V
    qk, v, gate_s, uc = _norm_proj(
        x3, S, norm_mix[1],
        [(w[:, :c0].astype(BF16), BF16, None, None),
         (w[:, c0:c1].astype(BF16), BF16, None, None),
         (w[:, c1:c2].astype(BF16), BF16, None, (("silu", 1.0),) * (RET_V // LANES)),
         (w[:, c2:].astype(BF16), BF16, (hy_conv_w[0], hy_conv_b[0]), None)])
    y_c = _retention(qk.reshape(B, S, -1), v.reshape(B, S, -1), gate_s.reshape(B, S, -1),
                     ret_log_decay[0], cos_t, sin_t)
    dft = _dft_matrix(S)
    spec = _hyena_spectrum(S, dft, hy_w1[0], hy_b1[0], hy_f1[0], hy_w2[0], hy_b2[0], hy_f2[0],
                           hy_w3[0])
    dft_b = dft.astype(BF16)
    y_d = _hyena(uc.reshape(B, S, -1), hy_bias[0], dft_b, jnp.swapaxes(dft_b, 1, 2), spec)
    wo = cd_w_out[0].astype(BF16)
    x2 = _out_proj(x3, y_c.reshape(T, -1), y_d.reshape(T, -1), wo[:RET_V], wo[RET_V:])
    out = _conv_ffn(x2, S, norm_ffn[1], ffn_w_in[1], ffn_conv_w[1], ffn_conv_b[1],
                    ffn_w_out[1], final_norm, True)
    return out.reshape(B, S, D)
```

```python
import functools
import math

import numpy as np
import jax
import jax.numpy as jnp
from jax import lax
from jax.experimental import pallas as pl
from jax.experimental.pallas import tpu as pltpu

F32 = jnp.float32
BF16 = jnp.bfloat16
HIGHEST = lax.Precision.HIGHEST

NORM_EPS = 1e-6
ROPE_THETA = 10000.0
NEG_INF = -1e30
LOG2E = 1.4426950408889634

D_MODEL = 1024
DN_HEADS = 4
DN_HEAD_DIM = 128
DN_WIDTH = DN_HEADS * DN_HEAD_DIM
DN_CHUNK = 64
DN_CHAINS = 2 * DN_HEADS
DN_PRE_CHUNKS = 4
DIL_HEADS = 8
DIL_HEAD_DIM = 64
DIL_WIDTH = DIL_HEADS * DIL_HEAD_DIM
DIL_PATTERNS = ((128, 1), (512, 4), (2048, 16))
DIL_QBLK = 128
DIL_UNROLL = 8
RET_HEADS = 4
RET_KEY_DIM = 64
RET_VAL_DIM = 128
RET_QK = RET_HEADS * RET_KEY_DIM
RET_V = RET_HEADS * RET_VAL_DIM
RET_CHUNK = 256
HY_WIDTH = 512
HY_EMB = 33
HY_ORDER = 64
HY_TARGET = 1e-2
HY_FAST = 0.3
HY_SLOW = 1.5
HY_FREQ_TILE = 256
D_FF = 2816
PROJ_CHUNK = 256
FF_ROWS = 1024
FF_SPLIT = 2
FF_CHUNK = 256

LANES = 128
VMEM_LIMIT = 56 * 1024 * 1024


def _cparams(*sem):
    return pltpu.CompilerParams(dimension_semantics=sem, vmem_limit_bytes=VMEM_LIMIT)


def _dot(a, b):
    return jnp.dot(a, b, preferred_element_type=F32)


def _dot_nt(a, b):
    return lax.dot_general(a, b, (((1,), (1,)), ((), ())), preferred_element_type=F32)


def _dot_tn(a, b):
    return lax.dot_general(a, b, (((0,), (0,)), ((), ())), preferred_element_type=F32)


def _silu(x):
    return x * (1.0 / (1.0 + jnp.exp(-x)))


def _rms(x, w):
    return x * lax.rsqrt(jnp.mean(x * x, axis=-1, keepdims=True) + NORM_EPS) * w


SUBLANES = 8


def _conv3_rows(x, w0, w1, w2, edge_prev=0.0, edge_next=0.0):
    n = x.shape[0]
    prev = pltpu.roll(x, 1, axis=0)
    nxt = pltpu.roll(x, n - 1, axis=0)
    y = prev * w0 + x * w1 + nxt * w2
    r8 = lax.broadcasted_iota(jnp.int32, (SUBLANES, x.shape[1]), 0)
    head = y[:SUBLANES] + jnp.where(r8 == 0, (edge_prev - prev[:SUBLANES]) * w0, 0.0)
    tail = y[n - SUBLANES:] + jnp.where(r8 == SUBLANES - 1, (edge_next - nxt[n - SUBLANES:]) * w2, 0.0)
    return jnp.concatenate([head, y[SUBLANES:n - SUBLANES], tail], axis=0)


def _proj_kernel(x_ref, xp_ref, xn_ref, nw_ref, *refs, groups, tiles_per_seq):
    i = pl.program_id(0)
    tm = x_ref.shape[0]
    nw = nw_ref[...]
    h = _rms(x_ref[...], nw)
    hb = h.astype(BF16)
    if any(g[0] for g in groups):
        hx = jnp.concatenate([_rms(xp_ref[...], nw), h, _rms(xn_ref[...], nw)], axis=0).astype(BF16)
        lo, hi = SUBLANES, SUBLANES + tm
        first = (i % tiles_per_seq) == 0
        last = (i % tiles_per_seq) == tiles_per_seq - 1
    n_in = sum(1 + int(g[0]) + int(g[1]) for g in groups)
    in_refs, out_refs = refs[:n_in], refs[n_in:]

    def chunk(group, o_ref, w_ref, cw_ref, cb_ref, cs):
        has_conv, has_bias, acts = group
        w = w_ref[:, cs]
        if has_conv:
            yx = _dot(hx, w)
            y = _conv3_rows(yx[lo:hi], cw_ref[0:1, cs], cw_ref[1:2, cs], cw_ref[2:3, cs],
                            jnp.where(first, 0.0, yx[lo - 1:lo]), jnp.where(last, 0.0, yx[hi:hi + 1]))
            if has_bias:
                y = y + cb_ref[:, cs]
        else:
            y = _dot(hb, w)
        for t0 in range(cs.start, cs.stop, LANES):
            yt = y[:, t0 - cs.start:t0 - cs.start + LANES]
            act = acts[t0 // LANES]
            if act is not None:
                yt = _silu(yt)
                if act[0] == "silu_l2":
                    yt = yt * (lax.rsqrt(jnp.sum(yt * yt, axis=-1, keepdims=True) + 1e-6) * act[1])
            o_ref[:, t0:t0 + LANES] = yt.astype(o_ref.dtype)

    heavy, light = [], []
    pos = 0
    for group, o_ref in zip(groups, out_refs):
        has_conv, has_bias, acts = group
        w_ref = in_refs[pos]
        cw_ref = in_refs[pos + 1] if has_conv else None
        cb_ref = in_refs[pos + 1 + int(has_conv)] if has_bias else None
        pos += 1 + int(has_conv) + int(has_bias)
        n_cols = w_ref.shape[1]
        for c0 in range(0, n_cols, PROJ_CHUNK):
            item = (group, o_ref, w_ref, cw_ref, cb_ref, slice(c0, min(c0 + PROJ_CHUNK, n_cols)))
            (heavy if has_conv or any(a is not None for a in acts) else light).append(item)
    while heavy or light:
        if heavy:
            chunk(*heavy.pop(0))
        if light:
            chunk(*light.pop(0))


def _norm_proj(x2d, seq_len, norm_w, outs, tm=512):
    T, D = x2d.shape
    nblk8 = T // 8
    const = lambda shp: pl.BlockSpec(shp, lambda i: (0, 0))
    in_specs = [pl.BlockSpec((tm, D), lambda i: (i, 0)),
                pl.BlockSpec((8, D), lambda i: (jnp.maximum(i * (tm // 8) - 1, 0), 0)),
                pl.BlockSpec((8, D), lambda i: (jnp.minimum((i + 1) * (tm // 8), nblk8 - 1), 0)),
                const((1, D))]
    args = [x2d, x2d, x2d, norm_w.reshape(1, D)]
    groups, out_specs, out_shape = [], [], []
    for w, dtype, conv, acts in outs:
        n = w.shape[1]
        args.append(w)
        in_specs.append(const(w.shape))
        if conv is not None:
            cw, cb = conv
            args.append(cw.astype(F32))
            in_specs.append(const((3, n)))
            if cb is not None:
                args.append(cb.astype(F32).reshape(1, n))
                in_specs.append(const((1, n)))
        acts = acts if acts is not None else (None,) * (n // LANES)
        groups.append((conv is not None, conv is not None and conv[1] is not None, tuple(acts)))
        out_specs.append(pl.BlockSpec((tm, n), lambda i: (i, 0)))
        out_shape.append(jax.ShapeDtypeStruct((T, n), dtype))
    return pl.pallas_call(
        functools.partial(_proj_kernel, groups=tuple(groups), tiles_per_seq=seq_len // tm),
        grid=(T // tm,), in_specs=in_specs, out_specs=out_specs,
        out_shape=out_shape, compiler_params=_cparams("parallel"), name="norm_proj",
    )(*args)


def _outproj_kernel(x_ref, ya_ref, yb_ref, wa_ref, wb_ref, o_ref):
    o_ref[...] = x_ref[...] + _dot(ya_ref[...], wa_ref[...]) + _dot(yb_ref[...], wb_ref[...])


def _out_proj(x2d, ya, yb, wa, wb, tm=1024):
    T, D = x2d.shape
    row = lambda i: (i, 0)
    full = lambda i: (0, 0)
    return pl.pallas_call(
        _outproj_kernel, grid=(T // tm,),
        in_specs=[pl.BlockSpec((tm, D), row), pl.BlockSpec((tm, ya.shape[1]), row),
                  pl.BlockSpec((tm, yb.shape[1]), row), pl.BlockSpec(wa.shape, full),
                  pl.BlockSpec(wb.shape, full)],
        out_specs=pl.BlockSpec((tm, D), row),
        out_shape=jax.ShapeDtypeStruct((T, D), F32),
        compiler_params=_cparams("parallel"), name="out_proj",
    )(x2d, ya, yb, wa, wb)


def _ffn_kernel(x_ref, xp_ref, xn_ref, nw_ref, wg_ref, wu_ref, cw_ref, cb_ref, wo_ref, fw_ref,
                o_ref, h_s, act_s, *, final_norm, tiles_per_seq):
    i = pl.program_id(0)
    j = pl.program_id(1)
    tm = x_ref.shape[0]
    tfo = wg_ref.shape[1]
    lo, hi = SUBLANES, SUBLANES + tm

    @pl.when(j == 0)
    def _():
        nw = nw_ref[...]
        h_s[...] = jnp.concatenate([_rms(xp_ref[...], nw), _rms(x_ref[...], nw),
                                    _rms(xn_ref[...], nw)], axis=0).astype(BF16)

    first = (i % tiles_per_seq) == 0
    last = (i % tiles_per_seq) == tiles_per_seq - 1
    hx = h_s[...]
    for c0 in range(0, tfo, FF_CHUNK):
        cs = slice(c0, min(c0 + FF_CHUNK, tfo))
        gate = _dot(hx, wg_ref[:, cs])
        up = _dot(hx, wu_ref[:, cs])
        g_prev = jnp.where(first, 0.0, gate[lo - 1:lo])
        g_next = jnp.where(last, 0.0, gate[hi:hi + 1])
        g = _conv3_rows(gate[lo:hi], cw_ref[0:1, cs], cw_ref[1:2, cs], cw_ref[2:3, cs],
                        g_prev, g_next) + cb_ref[:, cs]
        act_s[j, :, cs] = (_silu(g) * up[lo:hi]).astype(BF16)

    @pl.when(j == pl.num_programs(1) - 1)
    def _():
        y = x_ref[...]
        for jj in range(act_s.shape[0]):
            y = y + _dot(act_s[jj], wo_ref[jj * tfo:(jj + 1) * tfo, :])
        o_ref[...] = _rms(y, fw_ref[...]) if final_norm else y


def _conv_ffn(x2d, seq_len, norm_w, w_in, conv_w, conv_b, w_out, final_w, final_norm):
    T, D = x2d.shape
    tm = FF_ROWS
    nj = FF_SPLIT
    tfo = D_FF // nj
    nblk8 = T // 8
    wg = w_in[:, :D_FF].astype(BF16)
    wu = w_in[:, D_FF:].astype(BF16)
    wo = w_out.astype(BF16)
    return pl.pallas_call(
        functools.partial(_ffn_kernel, final_norm=final_norm, tiles_per_seq=seq_len // tm),
        grid=(T // tm, nj),
        in_specs=[pl.BlockSpec((tm, D), lambda i, j: (i, 0)),
                  pl.BlockSpec((8, D), lambda i, j: (jnp.maximum(i * (tm // 8) - 1, 0), 0)),
                  pl.BlockSpec((8, D), lambda i, j: (jnp.minimum((i + 1) * (tm // 8), nblk8 - 1), 0)),
                  pl.BlockSpec((1, D), lambda i, j: (0, 0)),
                  pl.BlockSpec((D, tfo), lambda i, j: (0, j)),
                  pl.BlockSpec((D, tfo), lambda i, j: (0, j)),
                  pl.BlockSpec((3, tfo), lambda i, j: (0, j)),
                  pl.BlockSpec((1, tfo), lambda i, j: (0, j)),
                  pl.BlockSpec((D_FF, D), lambda i, j: (0, 0)),
                  pl.BlockSpec((1, D), lambda i, j: (0, 0))],
        out_specs=pl.BlockSpec((tm, D), lambda i, j: (i, 0)),
        out_shape=jax.ShapeDtypeStruct((T, D), F32),
        scratch_shapes=[pltpu.VMEM((tm + 2 * SUBLANES, D), BF16), pltpu.VMEM((nj, tm, tfo), BF16)],
        compiler_params=_cparams("parallel", "arbitrary"), name="conv_ffn",
    )(x2d, x2d, x2d, norm_w.reshape(1, D), wg, wu, conv_w, conv_b.reshape(1, D_FF), wo,
      final_w.reshape(1, D))


def _deltanet_kernel(q_ref, k_ref, v_ref, zs_ref, ba_ref, gp_ref, nw_ref, o_ref,
                     pk_s, o_s, u_s, w_s, in_s, st_s):
    S = q_ref.shape[1]
    W, E, C, H = DN_WIDTH, DN_HEAD_DIM, DN_CHUNK, DN_HEADS
    n_chunks = S // C
    row = lax.broadcasted_iota(jnp.int32, (S, LANES), 0)
    lane = lax.broadcasted_iota(jnp.int32, (S, LANES), 1)
    q_s, k_s, v_s = q_ref.at[0], k_ref.at[0], v_ref.at[0]

    ba = ba_ref[0]
    beta = 1.0 / (1.0 + jnp.exp(-ba))
    xg = ba + gp_ref[1:2, :]
    g = gp_ref[0:1, :] * (jnp.maximum(xg, 0.0) + jnp.log1p(jnp.exp(-jnp.abs(xg))))
    pos = row & (C - 1)
    pre = g
    suf = g
    sh = 1
    while sh < C:
        pre = pre + jnp.where(pos >= sh, pltpu.roll(pre, sh, axis=0), 0.0)
        suf = suf + jnp.where(pos < C - sh, pltpu.roll(suf, S - sh, axis=0), 0.0)
        sh *= 2
    fwd = lane < 8 + H
    gc = jnp.where(fwd, pre, suf)
    glc = jnp.where(fwd, suf, pre) - g
    pk = jnp.where(lane < 8, beta,
                   jnp.where(lane < 16, gc,
                             jnp.where(lane < 24, pltpu.roll(glc, 8, axis=1),
                                       pltpu.roll(gc + glc, 16, axis=1))))
    pk_s[...] = pk

    st_s[...] = jnp.zeros_like(st_s)
    o_s[...] = jnp.zeros_like(o_s)

    ri = lax.broadcasted_iota(jnp.int32, (C, LANES), 0)
    ci = lax.broadcasted_iota(jnp.int32, (C, LANES), 1)
    lo = ci < C
    eye_hi = (ci == ri + C).astype(F32)
    zeros_b = jnp.zeros((C, E), BF16)

    heads = [slice(h * E, (h + 1) * E) for h in range(H)]
    chains = [(d, h) for d in range(2) for h in range(H)]


    def pre_body(i, carry):
        units = [(cc, d, h) for cc in range(DN_PRE_CHUNKS) for d, h in chains]
        rows, pkc, tr, k_c, v_c, s2 = [], [], [], [], [], []
        for cc in range(DN_PRE_CHUNKS):
            rw = pl.ds(pl.multiple_of((i * DN_PRE_CHUNKS + cc) * C, C), C)
            rows.append(rw)
            pk_c = pk_s[rw, :]
            pkc.append(pk_c)
            tr.append(jnp.transpose(jnp.concatenate([pk_c, pk_c], axis=0)))
            k_c.append([k_s[rw, hs] for hs in heads])
            v_c.append([v_s[rw, hs] for hs in heads])
            s2.append([_dot_nt(jnp.concatenate([q_s[rw, hs], k], axis=0),
                               jnp.concatenate([k, zeros_b], axis=0))
                       for hs, k in zip(heads, k_c[cc])])
        z, b_row, bg_row = [], [], []
        for cc, d, h in units:
            c = d * H + h
            causal = ((ri >= ci) if d == 0 else (ri <= ci)) & lo
            strict = ((ri > ci) if d == 0 else (ri < ci)) & lo
            beta_col = pkc[cc][:, c:c + 1]
            gc_col = pkc[cc][:, 8 + c:9 + c]
            beta_row = tr[cc][c:c + 1, :]
            gc_row = tr[cc][8 + c:9 + c, :]
            dmat = jnp.where(causal, jnp.exp(jnp.where(causal, gc_col - gc_row, 0.0)), 0.0)
            in_s[c, rows[cc], :] = (s2[cc][h][:C] * dmat).astype(BF16)
            z.append(eye_hi - jnp.where(strict, s2[cc][h][C:] * beta_col * dmat, 0.0))
            b_row.append(beta_row)
            bg_row.append(beta_row * jnp.exp(gc_row))
        for it in range(6):
            zb = [x.astype(BF16) for x in z]
            z = [_dot(b, jnp.concatenate([b, zeros_b], axis=0)) + jnp.where(lo, 0.0, x)
                 for b, x in zip(zb, z)]
        us = [_dot((x * br).astype(BF16), jnp.concatenate([zeros_b, v_c[cc][h]], axis=0))
              for (cc, d, h), x, br in zip(units, z, b_row)]
        ws = [_dot((x * br).astype(BF16), jnp.concatenate([zeros_b, k_c[cc][h]], axis=0))
              for (cc, d, h), x, br in zip(units, z, bg_row)]
        for (cc, d, h), u, w in zip(units, us, ws):
            u_s[d, rows[cc], heads[h]] = u.astype(BF16)
            w_s[d, rows[cc], heads[h]] = w.astype(BF16)
        return carry

    lax.fori_loop(0, n_chunks // DN_PRE_CHUNKS, pre_body, 0)

    def rec_body(n, carry):
        rows, ex = [], []
        for d in range(2):
            nd = n if d == 0 else n_chunks - 1 - n
            rows.append(pl.ds(pl.multiple_of(nd * C, C), C))
            ex.append(jnp.exp(pk_s[rows[d], :]))
        st = [st_s[d * H + h] for d, h in chains]
        wq = [_dot(jnp.concatenate([w_s[d, rows[d], heads[h]], q_s[rows[d], heads[h]]], axis=0),
                   s.astype(BF16)) for (d, h), s in zip(chains, st)]
        for (d, h), s, x in zip(chains, st, wq):
            c = d * H + h
            eg_col = ex[d][:, 8 + c:9 + c]
            eglc_col = ex[d][:, 16 + c:17 + c]
            etot_col = ex[d][:, 24 + c:25 + c]
            v_new = u_s[d, rows[d], heads[h]].astype(F32) - x[:C]
            o = eg_col * x[C:] + _dot(in_s[c, rows[d], :],
                                      jnp.concatenate([v_new.astype(BF16), zeros_b], axis=0))
            upd = _dot_tn(k_s[rows[d], heads[h]], (eglc_col * v_new).astype(BF16))
            st_s[c] = jnp.concatenate([etot_col, etot_col], axis=0) * s + upd
            o_s[rows[d], heads[h]] += o
        return carry

    lax.fori_loop(0, n_chunks, rec_body, 0)

    nw = nw_ref[...]
    for h in range(H):
        hs = slice(h * E, (h + 1) * E)
        o = o_s[:, hs]
        o = o * lax.rsqrt(jnp.mean(o * o, axis=-1, keepdims=True) + NORM_EPS) * nw
        o_ref[0, :, hs] = (o * zs_ref[0, :, hs].astype(F32)).astype(o_ref.dtype)


def _deltanet(qkv, zs, ba, a_log, dt_bias, norm_w):
    B, S, _ = qkv.shape
    W, E = DN_WIDTH, DN_HEAD_DIM
    gp = jnp.zeros((2, LANES), F32)
    gp = gp.at[0, 8:16].set(-jnp.exp(a_log.astype(F32)).reshape(-1))
    gp = gp.at[1, 8:16].set(dt_bias.astype(F32).reshape(-1))
    bspec = lambda n: pl.BlockSpec((1, S, n), lambda b: (b, 0, 0))
    full = lambda shp: pl.BlockSpec(shp, lambda b: (0,) * len(shp))
    return pl.pallas_call(
        _deltanet_kernel, grid=(B,),
        in_specs=[pl.BlockSpec((1, S, W), lambda b: (b, 0, 0)),
                  pl.BlockSpec((1, S, W), lambda b: (b, 0, 1)),
                  pl.BlockSpec((1, S, W), lambda b: (b, 0, 2)),
                  bspec(W), bspec(LANES), full((2, LANES)), full((1, E))],
        out_specs=bspec(W),
        out_shape=jax.ShapeDtypeStruct((B, S, W), BF16),
        scratch_shapes=[pltpu.VMEM((S, LANES), F32), pltpu.VMEM((S, W), F32),
                        pltpu.VMEM((2, S, W), BF16), pltpu.VMEM((2, S, W), BF16),
                        pltpu.VMEM((DN_CHAINS, S, LANES), BF16),
                        pltpu.VMEM((DN_CHAINS, E, E), F32)],
        compiler_params=_cparams("parallel"), name="deltanet",
    )(qkv, qkv, qkv, zs, ba, gp, norm_w.reshape(1, E).astype(F32))


def _rope_tables(S, E):
    inv = ROPE_THETA ** (-jnp.arange(0, E, 2, dtype=F32) / E)
    ang = jnp.arange(S, dtype=F32)[:, None] * inv[None, :]
    cos, sin = jnp.cos(ang), jnp.sin(ang)
    cos_t = jnp.tile(jnp.concatenate([cos, cos], axis=-1), (1, LANES // E))
    sin_t = jnp.tile(jnp.concatenate([-sin, sin], axis=-1), (1, LANES // E))
    return cos_t, sin_t


def _rope_tile(x, cos_t, sin_t, lane):
    half = DIL_HEAD_DIM // 2
    partner = jnp.where((lane & (DIL_HEAD_DIM - 1)) < half,
                        pltpu.roll(x, LANES - half, axis=1), pltpu.roll(x, half, axis=1))
    return x * cos_t + partner * sin_t


def _dilated_kernel(q_ref, k_ref, v_ref, cos_ref, sin_ref, o_ref,
                    q_s, k_s, v_s, num_s, den_s, max_s, bias_s):
    S = q_ref.shape[1]
    E = DIL_HEAD_DIM
    QB = DIL_QBLK
    lane = lax.broadcasted_iota(jnp.int32, (S, LANES), 1)
    cos_t = cos_ref[...]
    sin_t = sin_ref[...]
    q_s[...] = _rope_tile(q_ref[0].astype(F32), cos_t, sin_t, lane) * (E ** -0.5 * LOG2E)
    k_s[...] = _rope_tile(k_ref[0].astype(F32), cos_t, sin_t, lane)
    v_s[...] = v_ref[0].astype(F32)
    head0 = lax.broadcasted_iota(jnp.int32, (QB, LANES), 1) < E

    for p, (window, dil) in enumerate(DIL_PATTERNS):
        half = window // (2 * dil)
        L = S // dil
        nb = L // QB
        KW = min(QB + 2 * half, L)
        relm = (lax.broadcasted_iota(jnp.int32, (QB, KW), 1)
                - lax.broadcasted_iota(jnp.int32, (QB, KW), 0))
        for case, off in enumerate((0, -half, QB - KW)):
            bias_s[case, :, :KW] = jnp.where(jnp.abs(relm + off) <= half, 0.0, NEG_INF)
        ones_b = jnp.ones((KW, LANES), BF16)

        def block(i, carry, dil=dil, half=half, L=L, nb=nb, KW=KW, p=p, ones_b=ones_b):
            qrows, biases, vas, scores = [], [], [], []
            for uu in range(DIL_UNROLL):
                it = i * DIL_UNROLL + uu
                r = it // nb
                n = it % nb
                q0 = n * QB
                k0 = jnp.clip(q0 - half, 0, L - KW)
                if dil == 1:
                    qr = pl.ds(pl.multiple_of(q0, QB), QB)
                    kr = pl.ds(pl.multiple_of(k0, half), KW)
                else:
                    qr = pl.ds(r + dil * q0, QB, stride=dil)
                    kr = pl.ds(r + dil * k0, KW, stride=dil)
                qrows.append(qr)
                case = jnp.where(n == 0, 0, jnp.where(n == nb - 1, 2, 1))
                biases.append(bias_s[case, :, :KW])
                qf = q_s[qr, :]
                kb = k_s[kr, :].astype(BF16)
                vas.append(jnp.concatenate([v_s[kr, :].astype(BF16), ones_b], axis=1))
                for hh in range(2):
                    qh = jnp.where(head0 if hh == 0 else ~head0, qf, 0.0).astype(BF16)
                    scores.append(_dot_nt(qh, kb))
            ms, es = [], []
            for j, s in enumerate(scores):
                s = s + biases[j // 2]
                m = jnp.max(s, axis=-1, keepdims=True)
                ms.append(m)
                es.append(jnp.exp2(s - m).astype(BF16))
            oa = [_dot(e, vas[j // 2]) for j, e in enumerate(es)]
            for uu in range(DIL_UNROLL):
                a0, a1 = oa[2 * uu], oa[2 * uu + 1]
                num_s[p, qrows[uu], :] = jnp.where(head0, a0[:, :LANES], a1[:, :LANES])
                den_s[p, qrows[uu], :] = jnp.where(head0, a0[:, LANES:], a1[:, LANES:])
                max_s[p, qrows[uu], :] = jnp.where(head0, ms[2 * uu], ms[2 * uu + 1])
            return carry

        lax.fori_loop(0, dil * nb // DIL_UNROLL, block, 0)

    mx = jnp.maximum(jnp.maximum(max_s[0], max_s[1]), max_s[2])
    num = jnp.zeros((S, LANES), F32)
    den = jnp.zeros((S, LANES), F32)
    for p in range(len(DIL_PATTERNS)):
        w = jnp.exp2(max_s[p] - mx)
        num = num + w * num_s[p]
        den = den + w * den_s[p]
    o_ref[0] = (num * (1.0 / den)).astype(o_ref.dtype)


def _dilated_attention(qkv, cos_t, sin_t):
    B, S, _ = qkv.shape
    nt = DIL_WIDTH // LANES
    tile = lambda off: pl.BlockSpec((1, S, LANES), lambda b, t: (b, 0, t + off))
    tab = pl.BlockSpec((S, LANES), lambda b, t: (0, 0))
    return pl.pallas_call(
        _dilated_kernel, grid=(B, nt),
        in_specs=[tile(0), tile(nt), tile(2 * nt), tab, tab],
        out_specs=tile(0),
        out_shape=jax.ShapeDtypeStruct((B, S, DIL_WIDTH), BF16),
        scratch_shapes=[pltpu.VMEM((S, LANES), F32)] * 3
        + [pltpu.VMEM((len(DIL_PATTERNS), S, LANES), F32)] * 3
        + [pltpu.VMEM((3, DIL_QBLK, 2 * LANES), F32)],
        compiler_params=_cparams("parallel", "parallel"), name="dilated_attention",
    )(qkv, qkv, qkv, cos_t, sin_t)


def _retention_kernel(lg_ref, qk_ref, v_ref, g_ref, cos_ref, sin_ref, o_ref,
                      qk_s, d_s, kv_s, o_s):
    S = qk_ref.shape[1]
    H, EK, EV, C = RET_HEADS, RET_KEY_DIM, RET_VAL_DIM, RET_CHUNK
    n_chunks = S // C
    lane = lax.broadcasted_iota(jnp.int32, (S, LANES), 1)
    cos_t = cos_ref[...]
    sin_t = sin_ref[...]
    for t in range(2 * RET_QK // LANES):
        sl = slice(t * LANES, (t + 1) * LANES)
        y = _rope_tile(qk_ref[0, :, sl].astype(F32), cos_t, sin_t, lane)
        qk_s[:, sl] = y * (EK ** -0.5) if t < RET_QK // LANES else y

    rel = (lax.broadcasted_iota(jnp.int32, (C, C), 0)
           - lax.broadcasted_iota(jnp.int32, (C, C), 1)).astype(F32)
    idx = lax.broadcasted_iota(jnp.int32, (C, EK), 0).astype(F32)
    for h in range(H):
        lgf = lg_ref[0, h]
        lgb = lg_ref[1, h]
        d_s[h] = jnp.exp(lgf * jnp.maximum(rel, 0.0) + lgb * jnp.maximum(-rel, 0.0))
        k_dec = jnp.concatenate([jnp.exp(lgf * (C - 1 - idx)), jnp.exp(lgb * idx)], axis=1)
        q_dec = jnp.concatenate([jnp.exp(lgf * (idx + 1)), jnp.exp(lgb * (C - idx))], axis=1)
        qs = slice(h * EK, (h + 1) * EK)
        ks = slice(RET_QK + h * EK, RET_QK + (h + 1) * EK)
        vs = slice(h * EV, (h + 1) * EV)
        for n in range(n_chunks):
            rows = slice(n * C, (n + 1) * C)
            k_c = qk_s[rows, ks]
            k2 = (jnp.concatenate([k_c, k_c], axis=1) * k_dec).astype(BF16)
            kv_s[n] = _dot_tn(k2, v_ref[0, rows, vs])
        fdec = jnp.exp(lgf * C)
        bdec = jnp.exp(lgb * C)
        st = jnp.zeros((EK, EV), F32)
        for n in range(n_chunks):
            inc = kv_s[n, :EK, :]
            kv_s[n, :EK, :] = st
            st = st * fdec + inc
        st = jnp.zeros((EK, EV), F32)
        for n in range(n_chunks - 1, -1, -1):
            inc = kv_s[n, EK:, :]
            kv_s[n, EK:, :] = st
            st = st * bdec + inc
        for n in range(n_chunks):
            rows = slice(n * C, (n + 1) * C)
            q_c = qk_s[rows, qs]
            k_c = qk_s[rows, ks]
            sc = (_dot_nt(q_c.astype(BF16), k_c.astype(BF16)) * d_s[h]).astype(BF16)
            q2 = (jnp.concatenate([q_c, q_c], axis=1) * q_dec).astype(BF16)
            o_s[rows, vs] = _dot(sc, v_ref[0, rows, vs]) + _dot(q2, kv_s[n].astype(BF16))

    for h in range(H):
        vs = slice(h * EV, (h + 1) * EV)
        o = o_s[:, vs]
        o = o * lax.rsqrt(jnp.mean(o * o, axis=-1, keepdims=True) + NORM_EPS)
        o_ref[0, :, vs] = (o * g_ref[0, :, vs].astype(F32)).astype(o_ref.dtype)


def _retention(qk, v, gate, log_decay, cos_t, sin_t):
    B, S, _ = qk.shape
    H, EK, EV, C = RET_HEADS, RET_KEY_DIM, RET_VAL_DIM, RET_CHUNK
    bspec = lambda n: pl.BlockSpec((1, S, n), lambda b: (b, 0, 0))
    tab = pl.BlockSpec((S, LANES), lambda b: (0, 0))
    return pl.pallas_call(
        _retention_kernel, grid=(B,),
        in_specs=[pl.BlockSpec(memory_space=pltpu.SMEM), bspec(2 * RET_QK), bspec(RET_V),
                  bspec(RET_V), tab, tab],
        out_specs=bspec(RET_V),
        out_shape=jax.ShapeDtypeStruct((B, S, RET_V), BF16),
        scratch_shapes=[pltpu.VMEM((S, 2 * RET_QK), F32), pltpu.VMEM((H, C, C), F32),
                        pltpu.VMEM((S // C, 2 * EK, EV), F32), pltpu.VMEM((S, RET_V), F32)],
        compiler_params=_cparams("parallel"), name="retention",
    )(log_decay.astype(F32), qk, v, gate, cos_t, sin_t)


def _dft_matrix(S):
    n = 2 * S
    k = jnp.arange(S, dtype=jnp.int32)[:, None]
    t = jnp.arange(S, dtype=jnp.int32)[None, :]
    ang = ((k * t) % n).astype(F32) * (2.0 * math.pi / n)
    cosb = jnp.cos(ang)
    sinb = -jnp.sin(ang)
    nyq = jnp.where(t % 2 == 0, 1.0, -1.0).astype(F32)
    sinb = jnp.where(k == 0, nyq, sinb)
    return jnp.stack([cosb, sinb])


def _hy_filter_kernel(z_ref, w1_ref, b1_ref, f1_ref, w2_ref, b2_ref, f2_ref, w3_ref, dl_ref, o_ref):
    dot = functools.partial(jnp.dot, preferred_element_type=F32, precision=HIGHEST)
    z = z_ref[...]
    hid = jnp.sin(f1_ref[...] * (dot(z, w1_ref[...]) + b1_ref[...]))
    hid = jnp.sin(f2_ref[...] * (dot(hid, w2_ref[...]) + b2_ref[...]))
    o_ref[...] = dot(hid, w3_ref[...]) * jnp.exp(-z[:, 0:1] * dl_ref[...])


def _hy_spectrum_kernel(f_ref, h_ref, o_ref):
    o_ref[0] = jnp.dot(f_ref[0], h_ref[...], preferred_element_type=F32, precision=HIGHEST)


def _hyena_spectrum(S, dft, w1, b1, f1, w2, b2, f2, w3):
    t = jnp.linspace(0.0, 1.0, S, dtype=F32)[:, None]
    bands = (HY_EMB - 1) // 2
    wv = 2.0 * math.pi * jnp.arange(S, dtype=F32) / S
    fr = jnp.linspace(1e-4, bands - 1, bands, dtype=F32)
    ang = wv[:, None] * fr[None, :]
    z = jnp.concatenate([t, jnp.cos(ang), -jnp.sin(ang)], axis=-1)
    z = jnp.pad(z, ((0, 0), (0, LANES - HY_EMB)))
    w1p = jnp.pad(w1.astype(F32), ((0, LANES - HY_EMB), (0, 0)))
    deltas = jnp.abs(jnp.linspace(math.log(HY_TARGET) / HY_SLOW, math.log(HY_TARGET) / HY_FAST,
                                  HY_WIDTH, dtype=F32))
    dl = jnp.tile(deltas, 2)[None, :]
    row = lambda a: a.astype(F32).reshape(1, -1)
    filt = pl.pallas_call(
        _hy_filter_kernel, out_shape=jax.ShapeDtypeStruct((S, 2 * HY_WIDTH), F32),
        compiler_params=pltpu.CompilerParams(vmem_limit_bytes=VMEM_LIMIT), name="hyena_filter",
    )(z, w1p, row(b1), row(f1), w2.astype(F32), row(b2), row(f2), w3.astype(F32), dl)
    tf = HY_FREQ_TILE
    return pl.pallas_call(
        _hy_spectrum_kernel, grid=(2, S // tf),
        in_specs=[pl.BlockSpec((1, tf, S), lambda c, f: (c, f, 0)),
                  pl.BlockSpec((S, 2 * HY_WIDTH), lambda c, f: (0, 0))],
        out_specs=pl.BlockSpec((1, tf, 2 * HY_WIDTH), lambda c, f: (c, f, 0)),
        out_shape=jax.ShapeDtypeStruct((2, S, 2 * HY_WIDTH), F32),
        compiler_params=_cparams("parallel", "parallel"), name="hyena_spectrum",
    )(dft, filt)


def _hyena_kernel(x0_ref, x1_ref, v_ref, bias_ref, f_ref, ft_ref, sp_ref, o_ref, u_s, acc_s):
    f = pl.program_id(1)
    S = v_ref.shape[1]
    W = HY_WIDTH
    tf = f_ref.shape[1]

    @pl.when(f == 0)
    def _():
        u_s[...] = (v_ref[0].astype(F32) * x1_ref[0].astype(F32)).astype(BF16)
        acc_s[...] = jnp.zeros_like(acc_s)

    n = 2 * S
    sp_c = sp_ref[0]
    sp_s = sp_ref[1]
    s_re = sp_c[:, :W] + sp_c[:, W:]
    slot0 = (lax.broadcasted_iota(jnp.int32, (tf, W), 0) + f * tf) == 0
    wk = jnp.where(slot0, 1.0 / n, 2.0 / n)
    ca = s_re * wk
    cb2 = jnp.where(slot0, 0.0, (sp_s[:, :W] - sp_s[:, W:]) * wk)
    cd = jnp.where(slot0, (sp_s[:, :W] + sp_s[:, W:]) * wk, ca)

    u = u_s[...]
    u_re = _dot(f_ref[0], u)
    u_im = _dot(f_ref[1], u)
    y_re = (u_re * ca - u_im * cb2).astype(BF16)
    y_im = (u_re * cb2 + u_im * cd).astype(BF16)
    acc_s[...] += _dot(ft_ref[0], y_re) + _dot(ft_ref[1], y_im)

    @pl.when(f == pl.num_programs(1) - 1)
    def _():
        y = acc_s[...] + u_s[...].astype(F32) * bias_ref[...]
        o_ref[0] = (y * x0_ref[0].astype(F32)).astype(o_ref.dtype)


def _hyena(uc, bias, dft_b, dft_tb, spec):
    B, S, _ = uc.shape
    W = HY_WIDTH
    tf = HY_FREQ_TILE
    full = lambda shp: pl.BlockSpec(shp, lambda b, f: (0,) * len(shp))
    slab = lambda c: pl.BlockSpec((1, S, W), lambda b, f: (b, 0, c))
    return pl.pallas_call(
        _hyena_kernel, grid=(B, S // tf),
        in_specs=[slab(0), slab(1), slab(2), full((1, W)),
                  pl.BlockSpec((2, tf, S), lambda b, f: (0, f, 0)),
                  pl.BlockSpec((2, S, tf), lambda b, f: (0, 0, f)),
                  pl.BlockSpec((2, tf, 2 * W), lambda b, f: (0, f, 0))],
        out_specs=pl.BlockSpec((1, S, W), lambda b, f: (b, 0, 0)),
        out_shape=jax.ShapeDtypeStruct((B, S, W), BF16),
        scratch_shapes=[pltpu.VMEM((S, W), BF16), pltpu.VMEM((S, W), F32)],
        compiler_params=_cparams("parallel", "arbitrary"), name="hyena",
    )(uc, uc, uc, bias.astype(F32).reshape(1, -1), dft_b, dft_tb, spec)


def kernel(x, norm_mix, norm_ffn, final_norm, ab_w_in, dn_conv_w, dn_a_log, dn_dt_bias, dn_norm_w, ab_w_out, cd_w_in, ret_log_decay, hy_conv_w, hy_conv_b, hy_w1, hy_b1, hy_f1, hy_w2, hy_b2, hy_f2, hy_w3, hy_bias, cd_w_out, ffn_w_in, ffn_conv_w, ffn_conv_b, ffn_w_out):
    B, S, D = x.shape
    T = B * S
    x2 = x.reshape(T, D)
    cos_t, sin_t = _rope_tables(S, DIL_HEAD_DIM)

    w = ab_w_in[0]
    W = DN_WIDTH
    n_gate = 4 * DN_HEADS
    w_ba = jnp.pad(w[:, 4 * W:4 * W + n_gate], ((0, 0), (0, LANES - n_gate)))
    nh = DN_WIDTH // LANES
    qkv_acts = ((("silu_l2", DN_HEAD_DIM ** -0.5),) * nh + (("silu_l2", 1.0),) * nh
                + (("silu", 1.0),) * nh)
    qkv, zs, ba, dil = _norm_proj(
        x2, S, norm_mix[0],
        [(w[:, :3 * W].astype(BF16), BF16, (dn_conv_w[0], None), qkv_acts),
         (w[:, 3 * W:4 * W].astype(BF16), BF16, None, (("silu", 1.0),) * nh),
         (w_ba.astype(BF16), F32, None, None),
         (w[:, 4 * W + n_gate:].astype(BF16), BF16, None, None)])
    y_a = _deltanet(qkv.reshape(B, S, -1), zs.reshape(B, S, -1), ba.reshape(B, S, -1),
                    dn_a_log[0], dn_dt_bias[0], dn_norm_w[0])
    y_b = _dilated_attention(dil.reshape(B, S, -1), cos_t, sin_t)
    wo = ab_w_out[0].astype(BF16)
    x2 = _out_proj(x2, y_a.reshape(T, -1), y_b.reshape(T, -1), wo[:W], wo[W:])
    x3 = _conv_ffn(x2, S, norm_ffn[0], ffn_w_in[0], ffn_conv_w[0], ffn_conv_b[0],
                   ffn_w_out[0], final_norm, False)

    w = cd_w_in[0]
    c0, c1, c2 = 2 * RET_QK, 2 * RET_QK + RET_V, 2 * RET_QK + 2 * RET_V
    qk, v, gate_s, uc = _norm_proj(
        x3, S, norm_mix[1],
        [(w[:, :c0].astype(BF16), BF16, None, None),
         (w[:, c0:c1].astype(BF16), BF16, None, None),
         (w[:, c1:c2].astype(BF16), BF16, None, (("silu", 1.0),) * (RET_V // LANES)),
         (w[:, c2:].astype(BF16), BF16, (hy_conv_w[0], hy_conv_b[0]), None)])
    y_c = _retention(qk.reshape(B, S, -1), v.reshape(B, S, -1), gate_s.reshape(B, S, -1),
                     ret_log_decay[0], cos_t, sin_t)
    dft = _dft_matrix(S)
    spec = _hyena_spectrum(S, dft, hy_w1[0], hy_b1[0], hy_f1[0], hy_w2[0], hy_b2[0], hy_f2[0],
                           hy_w3[0])
    dft_b = dft.astype(BF16)
    y_d = _hyena(uc.reshape(B, S, -1), hy_bias[0], dft_b, jnp.swapaxes(dft_b, 1, 2), spec)
    wo = cd_w_out[0].astype(BF16)
    x2 = _out_proj(x3, y_c.reshape(T, -1), y_d.reshape(T, -1), wo[:RET_V], wo[RET_V:])
    out = _conv_ffn(x2, S, norm_ffn[1], ffn_w_in[1], ffn_conv_w[1], ffn_conv_b[1],
                    ffn_w_out[1], final_norm, True)
    return out.reshape(B, S, D)
```

```python
import functools
import math

import numpy as np
import jax
import jax.numpy as jnp
from jax import lax
from jax.experimental import pallas as pl
from jax.experimental.pallas import tpu as pltpu

F32 = jnp.float32
BF16 = jnp.bfloat16
HIGHEST = lax.Precision.HIGHEST

NORM_EPS = 1e-6
ROPE_THETA = 10000.0
NEG_INF = -1e30
LOG2E = 1.4426950408889634

D_MODEL = 1024
DN_HEADS = 4
DN_HEAD_DIM = 128
DN_WIDTH = DN_HEADS * DN_HEAD_DIM
DN_CHUNK = 64
DN_CHAINS = 2 * DN_HEADS
DN_PRE_CHUNKS = 4
DIL_HEADS = 8
DIL_HEAD_DIM = 64
DIL_WIDTH = DIL_HEADS * DIL_HEAD_DIM
DIL_PATTERNS = ((128, 1), (512, 4), (2048, 16))
DIL_QBLK = 128
DIL_UNROLL = 16
RET_HEADS = 4
RET_KEY_DIM = 64
RET_VAL_DIM = 128
RET_QK = RET_HEADS * RET_KEY_DIM
RET_V = RET_HEADS * RET_VAL_DIM
RET_CHUNK = 256
HY_WIDTH = 512
HY_EMB = 33
HY_ORDER = 64
HY_TARGET = 1e-2
HY_FAST = 0.3
HY_SLOW = 1.5
HY_FREQ_TILE = 512
D_FF = 2816
PROJ_CHUNK = 256
FF_ROWS = 1024
FF_SPLIT = 2
FF_CHUNK = 256

LANES = 128
VMEM_LIMIT = 56 * 1024 * 1024


def _cparams(*sem):
    return pltpu.CompilerParams(dimension_semantics=sem, vmem_limit_bytes=VMEM_LIMIT)


def _dot(a, b):
    return jnp.dot(a, b, preferred_element_type=F32)


def _dot_nt(a, b):
    return lax.dot_general(a, b, (((1,), (1,)), ((), ())), preferred_element_type=F32)


def _dot_tn(a, b):
    return lax.dot_general(a, b, (((0,), (0,)), ((), ())), preferred_element_type=F32)


def _silu(x):
    return x * (1.0 / (1.0 + jnp.exp(-x)))


def _rms(x, w):
    return x * lax.rsqrt(jnp.mean(x * x, axis=-1, keepdims=True) + NORM_EPS) * w


SUBLANES = 8


def _conv3_rows(x, w0, w1, w2, edge_prev=0.0, edge_next=0.0):
    n = x.shape[0]
    prev = pltpu.roll(x, 1, axis=0)
    nxt = pltpu.roll(x, n - 1, axis=0)
    y = prev * w0 + x * w1 + nxt * w2
    r8 = lax.broadcasted_iota(jnp.int32, (SUBLANES, x.shape[1]), 0)
    head = y[:SUBLANES] + jnp.where(r8 == 0, (edge_prev - prev[:SUBLANES]) * w0, 0.0)
    tail = y[n - SUBLANES:] + jnp.where(r8 == SUBLANES - 1, (edge_next - nxt[n - SUBLANES:]) * w2, 0.0)
    return jnp.concatenate([head, y[SUBLANES:n - SUBLANES], tail], axis=0)


def _proj_kernel(x_ref, xp_ref, xn_ref, nw_ref, *refs, groups, tiles_per_seq):
    i = pl.program_id(0)
    tm = x_ref.shape[0]
    nw = nw_ref[...]
    h = _rms(x_ref[...], nw)
    hb = h.astype(BF16)
    if any(g[0] for g in groups):
        hx = jnp.concatenate([_rms(xp_ref[...], nw), h, _rms(xn_ref[...], nw)], axis=0).astype(BF16)
        lo, hi = SUBLANES, SUBLANES + tm
        first = (i % tiles_per_seq) == 0
        last = (i % tiles_per_seq) == tiles_per_seq - 1
    n_in = sum(1 + int(g[0]) + int(g[1]) for g in groups)
    in_refs, out_refs = refs[:n_in], refs[n_in:]

    def chunk(group, o_ref, w_ref, cw_ref, cb_ref, cs):
        has_conv, has_bias, acts = group
        w = w_ref[:, cs]
        if has_conv:
            yx = _dot(hx, w)
            y = _conv3_rows(yx[lo:hi], cw_ref[0:1, cs], cw_ref[1:2, cs], cw_ref[2:3, cs],
                            jnp.where(first, 0.0, yx[lo - 1:lo]), jnp.where(last, 0.0, yx[hi:hi + 1]))
            if has_bias:
                y = y + cb_ref[:, cs]
        else:
            y = _dot(hb, w)
        for t0 in range(cs.start, cs.stop, LANES):
            yt = y[:, t0 - cs.start:t0 - cs.start + LANES]
            act = acts[t0 // LANES]
            if act is not None:
                yt = _silu(yt)
                if act[0] == "silu_l2":
                    yt = yt * (lax.rsqrt(jnp.sum(yt * yt, axis=-1, keepdims=True) + 1e-6) * act[1])
            o_ref[:, t0:t0 + LANES] = yt.astype(o_ref.dtype)

    heavy, light = [], []
    pos = 0
    for group, o_ref in zip(groups, out_refs):
        has_conv, has_bias, acts = group
        w_ref = in_refs[pos]
        cw_ref = in_refs[pos + 1] if has_conv else None
        cb_ref = in_refs[pos + 1 + int(has_conv)] if has_bias else None
        pos += 1 + int(has_conv) + int(has_bias)
        n_cols = w_ref.shape[1]
        for c0 in range(0, n_cols, PROJ_CHUNK):
            item = (group, o_ref, w_ref, cw_ref, cb_ref, slice(c0, min(c0 + PROJ_CHUNK, n_cols)))
            (heavy if has_conv or any(a is not None for a in acts) else light).append(item)
    while heavy or light:
        if heavy:
            chunk(*heavy.pop(0))
        if light:
            chunk(*light.pop(0))


def _norm_proj(x2d, seq_len, norm_w, outs, tm=512):
    T, D = x2d.shape
    nblk8 = T // 8
    const = lambda shp: pl.BlockSpec(shp, lambda i: (0, 0))
    in_specs = [pl.BlockSpec((tm, D), lambda i: (i, 0)),
                pl.BlockSpec((8, D), lambda i: (jnp.maximum(i * (tm // 8) - 1, 0), 0)),
                pl.BlockSpec((8, D), lambda i: (jnp.minimum((i + 1) * (tm // 8), nblk8 - 1), 0)),
                const((1, D))]
    args = [x2d, x2d, x2d, norm_w.reshape(1, D)]
    groups, out_specs, out_shape = [], [], []
    for w, dtype, conv, acts in outs:
        n = w.shape[1]
        args.append(w)
        in_specs.append(const(w.shape))
        if conv is not None:
            cw, cb = conv
            args.append(cw.astype(F32))
            in_specs.append(const((3, n)))
            if cb is not None:
                args.append(cb.astype(F32).reshape(1, n))
                in_specs.append(const((1, n)))
        acts = acts if acts is not None else (None,) * (n // LANES)
        groups.append((conv is not None, conv is not None and conv[1] is not None, tuple(acts)))
        out_specs.append(pl.BlockSpec((tm, n), lambda i: (i, 0)))
        out_shape.append(jax.ShapeDtypeStruct((T, n), dtype))
    return pl.pallas_call(
        functools.partial(_proj_kernel, groups=tuple(groups), tiles_per_seq=seq_len // tm),
        grid=(T // tm,), in_specs=in_specs, out_specs=out_specs,
        out_shape=out_shape, compiler_params=_cparams("parallel"), name="norm_proj",
    )(*args)


def _outproj_kernel(x_ref, ya_ref, yb_ref, wa_ref, wb_ref, o_ref):
    o_ref[...] = x_ref[...] + _dot(ya_ref[...], wa_ref[...]) + _dot(yb_ref[...], wb_ref[...])


def _out_proj(x2d, ya, yb, wa, wb, tm=1024):
    T, D = x2d.shape
    row = lambda i: (i, 0)
    full = lambda i: (0, 0)
    return pl.pallas_call(
        _outproj_kernel, grid=(T // tm,),
        in_specs=[pl.BlockSpec((tm, D), row), pl.BlockSpec((tm, ya.shape[1]), row),
                  pl.BlockSpec((tm, yb.shape[1]), row), pl.BlockSpec(wa.shape, full),
                  pl.BlockSpec(wb.shape, full)],
        out_specs=pl.BlockSpec((tm, D), row),
        out_shape=jax.ShapeDtypeStruct((T, D), F32),
        compiler_params=_cparams("parallel"), name="out_proj",
    )(x2d, ya, yb, wa, wb)


def _ffn_kernel(x_ref, xp_ref, xn_ref, nw_ref, wg_ref, wu_ref, cw_ref, cb_ref, wo_ref, fw_ref,
                o_ref, h_s, act_s, *, final_norm, tiles_per_seq):
    i = pl.program_id(0)
    j = pl.program_id(1)
    tm = x_ref.shape[0]
    tfo = wg_ref.shape[1]
    lo, hi = SUBLANES, SUBLANES + tm

    @pl.when(j == 0)
    def _():
        nw = nw_ref[...]
        h_s[...] = jnp.concatenate([_rms(xp_ref[...], nw), _rms(x_ref[...], nw),
                                    _rms(xn_ref[...], nw)], axis=0).astype(BF16)

    first = (i % tiles_per_seq) == 0
    last = (i % tiles_per_seq) == tiles_per_seq - 1
    hx = h_s[...]
    for c0 in range(0, tfo, FF_CHUNK):
        cs = slice(c0, min(c0 + FF_CHUNK, tfo))
        gate = _dot(hx, wg_ref[:, cs])
        up = _dot(hx, wu_ref[:, cs])
        g_prev = jnp.where(first, 0.0, gate[lo - 1:lo])
        g_next = jnp.where(last, 0.0, gate[hi:hi + 1])
        g = _conv3_rows(gate[lo:hi], cw_ref[0:1, cs], cw_ref[1:2, cs], cw_ref[2:3, cs],
                        g_prev, g_next) + cb_ref[:, cs]
        act_s[j, :, cs] = (_silu(g) * up[lo:hi]).astype(BF16)

    @pl.when(j == pl.num_programs(1) - 1)
    def _():
        y = x_ref[...]
        for jj in range(act_s.shape[0]):
            y = y + _dot(act_s[jj], wo_ref[jj * tfo:(jj + 1) * tfo, :])
        o_ref[...] = _rms(y, fw_ref[...]) if final_norm else y


def _conv_ffn(x2d, seq_len, norm_w, w_in, conv_w, conv_b, w_out, final_w, final_norm):
    T, D = x2d.shape
    tm = FF_ROWS
    nj = FF_SPLIT
    tfo = D_FF // nj
    nblk8 = T // 8
    wg = w_in[:, :D_FF].astype(BF16)
    wu = w_in[:, D_FF:].astype(BF16)
    wo = w_out.astype(BF16)
    return pl.pallas_call(
        functools.partial(_ffn_kernel, final_norm=final_norm, tiles_per_seq=seq_len // tm),
        grid=(T // tm, nj),
        in_specs=[pl.BlockSpec((tm, D), lambda i, j: (i, 0)),
                  pl.BlockSpec((8, D), lambda i, j: (jnp.maximum(i * (tm // 8) - 1, 0), 0)),
                  pl.BlockSpec((8, D), lambda i, j: (jnp.minimum((i + 1) * (tm // 8), nblk8 - 1), 0)),
                  pl.BlockSpec((1, D), lambda i, j: (0, 0)),
                  pl.BlockSpec((D, tfo), lambda i, j: (0, j)),
                  pl.BlockSpec((D, tfo), lambda i, j: (0, j)),
                  pl.BlockSpec((3, tfo), lambda i, j: (0, j)),
                  pl.BlockSpec((1, tfo), lambda i, j: (0, j)),
                  pl.BlockSpec((D_FF, D), lambda i, j: (0, 0)),
                  pl.BlockSpec((1, D), lambda i, j: (0, 0))],
        out_specs=pl.BlockSpec((tm, D), lambda i, j: (i, 0)),
        out_shape=jax.ShapeDtypeStruct((T, D), F32),
        scratch_shapes=[pltpu.VMEM((tm + 2 * SUBLANES, D), BF16), pltpu.VMEM((nj, tm, tfo), BF16)],
        compiler_params=_cparams("parallel", "arbitrary"), name="conv_ffn",
    )(x2d, x2d, x2d, norm_w.reshape(1, D), wg, wu, conv_w, conv_b.reshape(1, D_FF), wo,
      final_w.reshape(1, D))


def _deltanet_kernel(q_ref, k_ref, v_ref, zs_ref, ba_ref, gp_ref, nw_ref, o_ref,
                     pk_s, o_s, u_s, w_s, in_s, st_s):
    S = q_ref.shape[1]
    W, E, C, H = DN_WIDTH, DN_HEAD_DIM, DN_CHUNK, DN_HEADS
    n_chunks = S // C
    row = lax.broadcasted_iota(jnp.int32, (S, LANES), 0)
    lane = lax.broadcasted_iota(jnp.int32, (S, LANES), 1)
    q_s, k_s, v_s = q_ref.at[0], k_ref.at[0], v_ref.at[0]

    ba = ba_ref[0]
    beta = 1.0 / (1.0 + jnp.exp(-ba))
    xg = ba + gp_ref[1:2, :]
    g = gp_ref[0:1, :] * (jnp.maximum(xg, 0.0) + jnp.log1p(jnp.exp(-jnp.abs(xg))))
    pos = row & (C - 1)
    pre = g
    suf = g
    sh = 1
    while sh < C:
        pre = pre + jnp.where(pos >= sh, pltpu.roll(pre, sh, axis=0), 0.0)
        suf = suf + jnp.where(pos < C - sh, pltpu.roll(suf, S - sh, axis=0), 0.0)
        sh *= 2
    fwd = lane < 8 + H
    gc = jnp.where(fwd, pre, suf)
    glc = jnp.where(fwd, suf, pre) - g
    pk = jnp.where(lane < 8, beta,
                   jnp.where(lane < 16, gc,
                             jnp.where(lane < 24, pltpu.roll(glc, 8, axis=1),
                                       pltpu.roll(gc + glc, 16, axis=1))))
    pk_s[...] = pk

    st_s[...] = jnp.zeros_like(st_s)
    o_s[...] = jnp.zeros_like(o_s)

    ri = lax.broadcasted_iota(jnp.int32, (C, LANES), 0)
    ci = lax.broadcasted_iota(jnp.int32, (C, LANES), 1)
    lo = ci < C
    eye_hi = (ci == ri + C).astype(F32)
    zeros_b = jnp.zeros((C, E), BF16)

    heads = [slice(h * E, (h + 1) * E) for h in range(H)]
    chains = [(d, h) for d in range(2) for h in range(H)]


    def pre_body(i, carry):
        units = [(cc, d, h) for cc in range(DN_PRE_CHUNKS) for d, h in chains]
        rows, pkc, tr, k_c, v_c, s2 = [], [], [], [], [], []
        for cc in range(DN_PRE_CHUNKS):
            rw = pl.ds(pl.multiple_of((i * DN_PRE_CHUNKS + cc) * C, C), C)
            rows.append(rw)
            pk_c = pk_s[rw, :]
            pkc.append(pk_c)
            tr.append(jnp.transpose(jnp.concatenate([pk_c, pk_c], axis=0)))
            k_c.append([k_s[rw, hs] for hs in heads])
            v_c.append([v_s[rw, hs] for hs in heads])
            s2.append([_dot_nt(jnp.concatenate([q_s[rw, hs], k], axis=0),
                               jnp.concatenate([k, zeros_b], axis=0))
                       for hs, k in zip(heads, k_c[cc])])
        z, b_row, bg_row = [], [], []
        for cc, d, h in units:
            c = d * H + h
            causal = ((ri >= ci) if d == 0 else (ri <= ci)) & lo
            strict = ((ri > ci) if d == 0 else (ri < ci)) & lo
            beta_col = pkc[cc][:, c:c + 1]
            gc_col = pkc[cc][:, 8 + c:9 + c]
            beta_row = tr[cc][c:c + 1, :]
            gc_row = tr[cc][8 + c:9 + c, :]
            dmat = jnp.where(causal, jnp.exp(jnp.where(causal, gc_col - gc_row, 0.0)), 0.0)
            in_s[c, rows[cc], :] = (s2[cc][h][:C] * dmat).astype(BF16)
            z.append(eye_hi - jnp.where(strict, s2[cc][h][C:] * beta_col * dmat, 0.0))
            b_row.append(beta_row)
            bg_row.append(beta_row * jnp.exp(gc_row))
        for it in range(6):
            zb = [x.astype(BF16) for x in z]
            z = [_dot(b, jnp.concatenate([b, zeros_b], axis=0)) + jnp.where(lo, 0.0, x)
                 for b, x in zip(zb, z)]
        us = [_dot((x * br).astype(BF16), jnp.concatenate([zeros_b, v_c[cc][h]], axis=0))
              for (cc, d, h), x, br in zip(units, z, b_row)]
        ws = [_dot((x * br).astype(BF16), jnp.concatenate([zeros_b, k_c[cc][h]], axis=0))
              for (cc, d, h), x, br in zip(units, z, bg_row)]
        for (cc, d, h), u, w in zip(units, us, ws):
            u_s[d, rows[cc], heads[h]] = u.astype(BF16)
            w_s[d, rows[cc], heads[h]] = w.astype(BF16)
        return carry

    lax.fori_loop(0, n_chunks // DN_PRE_CHUNKS, pre_body, 0)

    def rec_body(n, carry):
        rows, ex = [], []
        for d in range(2):
            nd = n if d == 0 else n_chunks - 1 - n
            rows.append(pl.ds(pl.multiple_of(nd * C, C), C))
            ex.append(jnp.exp(pk_s[rows[d], :]))
        st = [st_s[d * H + h] for d, h in chains]
        wq = [_dot(jnp.concatenate([w_s[d, rows[d], heads[h]], q_s[rows[d], heads[h]]], axis=0),
                   s.astype(BF16)) for (d, h), s in zip(chains, st)]
        for (d, h), s, x in zip(chains, st, wq):
            c = d * H + h
            eg_col = ex[d][:, 8 + c:9 + c]
            eglc_col = ex[d][:, 16 + c:17 + c]
            etot_col = ex[d][:, 24 + c:25 + c]
            v_new = u_s[d, rows[d], heads[h]].astype(F32) - x[:C]
            o = eg_col * x[C:] + _dot(in_s[c, rows[d], :],
                                      jnp.concatenate([v_new.astype(BF16), zeros_b], axis=0))
            upd = _dot_tn(k_s[rows[d], heads[h]], (eglc_col * v_new).astype(BF16))
            st_s[c] = jnp.concatenate([etot_col, etot_col], axis=0) * s + upd
            o_s[rows[d], heads[h]] += o
        return carry

    lax.fori_loop(0, n_chunks, rec_body, 0)

    nw = nw_ref[...]
    for h in range(H):
        hs = slice(h * E, (h + 1) * E)
        o = o_s[:, hs]
        o = o * lax.rsqrt(jnp.mean(o * o, axis=-1, keepdims=True) + NORM_EPS) * nw
        o_ref[0, :, hs] = (o * zs_ref[0, :, hs].astype(F32)).astype(o_ref.dtype)


def _deltanet(qkv, zs, ba, a_log, dt_bias, norm_w):
    B, S, _ = qkv.shape
    W, E = DN_WIDTH, DN_HEAD_DIM
    gp = jnp.zeros((2, LANES), F32)
    gp = gp.at[0, 8:16].set(-jnp.exp(a_log.astype(F32)).reshape(-1))
    gp = gp.at[1, 8:16].set(dt_bias.astype(F32).reshape(-1))
    bspec = lambda n: pl.BlockSpec((1, S, n), lambda b: (b, 0, 0))
    full = lambda shp: pl.BlockSpec(shp, lambda b: (0,) * len(shp))
    return pl.pallas_call(
        _deltanet_kernel, grid=(B,),
        in_specs=[pl.BlockSpec((1, S, W), lambda b: (b, 0, 0)),
                  pl.BlockSpec((1, S, W), lambda b: (b, 0, 1)),
                  pl.BlockSpec((1, S, W), lambda b: (b, 0, 2)),
                  bspec(W), bspec(LANES), full((2, LANES)), full((1, E))],
        out_specs=bspec(W),
        out_shape=jax.ShapeDtypeStruct((B, S, W), BF16),
        scratch_shapes=[pltpu.VMEM((S, LANES), F32), pltpu.VMEM((S, W), F32),
                        pltpu.VMEM((2, S, W), BF16), pltpu.VMEM((2, S, W), BF16),
                        pltpu.VMEM((DN_CHAINS, S, LANES), BF16),
                        pltpu.VMEM((DN_CHAINS, E, E), F32)],
        compiler_params=_cparams("parallel"), name="deltanet",
    )(qkv, qkv, qkv, zs, ba, gp, norm_w.reshape(1, E).astype(F32))


def _rope_tables(S, E):
    inv = ROPE_THETA ** (-jnp.arange(0, E, 2, dtype=F32) / E)
    ang = jnp.arange(S, dtype=F32)[:, None] * inv[None, :]
    cos, sin = jnp.cos(ang), jnp.sin(ang)
    cos_t = jnp.tile(jnp.concatenate([cos, cos], axis=-1), (1, LANES // E))
    sin_t = jnp.tile(jnp.concatenate([-sin, sin], axis=-1), (1, LANES // E))
    return cos_t, sin_t


def _rope_tile(x, cos_t, sin_t, lane):
    half = DIL_HEAD_DIM // 2
    partner = jnp.where((lane & (DIL_HEAD_DIM - 1)) < half,
                        pltpu.roll(x, LANES - half, axis=1), pltpu.roll(x, half, axis=1))
    return x * cos_t + partner * sin_t


def _dilated_kernel(q_ref, k_ref, v_ref, cos_ref, sin_ref, o_ref,
                    q_s, k_s, v_s, num_s, den_s, max_s, bias_s):
    S = q_ref.shape[1]
    E = DIL_HEAD_DIM
    QB = DIL_QBLK
    lane = lax.broadcasted_iota(jnp.int32, (S, LANES), 1)
    cos_t = cos_ref[...]
    sin_t = sin_ref[...]
    q_s[...] = _rope_tile(q_ref[0].astype(F32), cos_t, sin_t, lane) * (E ** -0.5 * LOG2E)
    k_s[...] = _rope_tile(k_ref[0].astype(F32), cos_t, sin_t, lane)
    v_s[...] = v_ref[0].astype(F32)
    head0 = lax.broadcasted_iota(jnp.int32, (QB, LANES), 1) < E

    for p, (window, dil) in enumerate(DIL_PATTERNS):
        half = window // (2 * dil)
        L = S // dil
        nb = L // QB
        KW = min(QB + 2 * half, L)
        relm = (lax.broadcasted_iota(jnp.int32, (QB, KW), 1)
                - lax.broadcasted_iota(jnp.int32, (QB, KW), 0))
        for case, off in enumerate((0, -half, QB - KW)):
            bias_s[case, :, :KW] = jnp.where(jnp.abs(relm + off) <= half, 0.0, NEG_INF)
        ones_b = jnp.ones((KW, LANES), BF16)

        def block(i, carry, dil=dil, half=half, L=L, nb=nb, KW=KW, p=p, ones_b=ones_b):
            qrows, biases, vas, scores = [], [], [], []
            for uu in range(DIL_UNROLL):
                it = i * DIL_UNROLL + uu
                r = it // nb
                n = it % nb
                q0 = n * QB
                k0 = jnp.clip(q0 - half, 0, L - KW)
                if dil == 1:
                    qr = pl.ds(pl.multiple_of(q0, QB), QB)
                    kr = pl.ds(pl.multiple_of(k0, half), KW)
                else:
                    qr = pl.ds(r + dil * q0, QB, stride=dil)
                    kr = pl.ds(r + dil * k0, KW, stride=dil)
                qrows.append(qr)
                case = jnp.where(n == 0, 0, jnp.where(n == nb - 1, 2, 1))
                biases.append(bias_s[case, :, :KW])
                qf = q_s[qr, :]
                kb = k_s[kr, :].astype(BF16)
                vas.append(jnp.concatenate([v_s[kr, :].astype(BF16), ones_b], axis=1))
                for hh in range(2):
                    qh = jnp.where(head0 if hh == 0 else ~head0, qf, 0.0).astype(BF16)
                    scores.append(_dot_nt(qh, kb))
            ms, es = [], []
            for j, s in enumerate(scores):
                s = s + biases[j // 2]
                m = jnp.max(s, axis=-1, keepdims=True)
                ms.append(m)
                es.append(jnp.exp2(s - m).astype(BF16))
            oa = [_dot(e, vas[j // 2]) for j, e in enumerate(es)]
            for uu in range(DIL_UNROLL):
                a0, a1 = oa[2 * uu], oa[2 * uu + 1]
                num_s[p, qrows[uu], :] = jnp.where(head0, a0[:, :LANES], a1[:, :LANES])
                den_s[p, qrows[uu], :] = jnp.where(head0, a0[:, LANES:], a1[:, LANES:])
                max_s[p, qrows[uu], :] = jnp.where(head0, ms[2 * uu], ms[2 * uu + 1])
            return carry

        lax.fori_loop(0, dil * nb // DIL_UNROLL, block, 0)

    mx = jnp.maximum(jnp.maximum(max_s[0], max_s[1]), max_s[2])
    num = jnp.zeros((S, LANES), F32)
    den = jnp.zeros((S, LANES), F32)
    for p in range(len(DIL_PATTERNS)):
        w = jnp.exp2(max_s[p] - mx)
        num = num + w * num_s[p]
        den = den + w * den_s[p]
    o_ref[0] = (num * (1.0 / den)).astype(o_ref.dtype)


def _dilated_attention(qkv, cos_t, sin_t):
    B, S, _ = qkv.shape
    nt = DIL_WIDTH // LANES
    tile = lambda off: pl.BlockSpec((1, S, LANES), lambda b, t: (b, 0, t + off))
    tab = pl.BlockSpec((S, LANES), lambda b, t: (0, 0))
    return pl.pallas_call(
        _dilated_kernel, grid=(B, nt),
        in_specs=[tile(0), tile(nt), tile(2 * nt), tab, tab],
        out_specs=tile(0),
        out_shape=jax.ShapeDtypeStruct((B, S, DIL_WIDTH), BF16),
        scratch_shapes=[pltpu.VMEM((S, LANES), F32)] * 3
        + [pltpu.VMEM((len(DIL_PATTERNS), S, LANES), F32)] * 3
        + [pltpu.VMEM((3, DIL_QBLK, 2 * LANES), F32)],
        compiler_params=_cparams("parallel", "parallel"), name="dilated_attention",
    )(qkv, qkv, qkv, cos_t, sin_t)


def _retention_kernel(lg_ref, qk_ref, v_ref, g_ref, cos_ref, sin_ref, o_ref,
                      qk_s, d_s, kv_s, o_s):
    S = qk_ref.shape[1]
    H, EK, EV, C = RET_HEADS, RET_KEY_DIM, RET_VAL_DIM, RET_CHUNK
    n_chunks = S // C
    lane = lax.broadcasted_iota(jnp.int32, (S, LANES), 1)
    cos_t = cos_ref[...]
    sin_t = sin_ref[...]
    for t in range(2 * RET_QK // LANES):
        sl = slice(t * LANES, (t + 1) * LANES)
        y = _rope_tile(qk_ref[0, :, sl].astype(F32), cos_t, sin_t, lane)
        qk_s[:, sl] = y * (EK ** -0.5) if t < RET_QK // LANES else y

    rel = (lax.broadcasted_iota(jnp.int32, (C, C), 0)
           - lax.broadcasted_iota(jnp.int32, (C, C), 1)).astype(F32)
    idx = lax.broadcasted_iota(jnp.int32, (C, EK), 0).astype(F32)
    for h in range(H):
        lgf = lg_ref[0, h]
        lgb = lg_ref[1, h]
        d_s[h] = jnp.exp(lgf * jnp.maximum(rel, 0.0) + lgb * jnp.maximum(-rel, 0.0))
        k_dec = jnp.concatenate([jnp.exp(lgf * (C - 1 - idx)), jnp.exp(lgb * idx)], axis=1)
        q_dec = jnp.concatenate([jnp.exp(lgf * (idx + 1)), jnp.exp(lgb * (C - idx))], axis=1)
        qs = slice(h * EK, (h + 1) * EK)
        ks = slice(RET_QK + h * EK, RET_QK + (h + 1) * EK)
        vs = slice(h * EV, (h + 1) * EV)
        for n in range(n_chunks):
            rows = slice(n * C, (n + 1) * C)
            k_c = qk_s[rows, ks]
            k2 = (jnp.concatenate([k_c, k_c], axis=1) * k_dec).astype(BF16)
            kv_s[n] = _dot_tn(k2, v_ref[0, rows, vs])
        fdec = jnp.exp(lgf * C)
        bdec = jnp.exp(lgb * C)
        st = jnp.zeros((EK, EV), F32)
        for n in range(n_chunks):
            inc = kv_s[n, :EK, :]
            kv_s[n, :EK, :] = st
            st = st * fdec + inc
        st = jnp.zeros((EK, EV), F32)
        for n in range(n_chunks - 1, -1, -1):
            inc = kv_s[n, EK:, :]
            kv_s[n, EK:, :] = st
            st = st * bdec + inc
        for n in range(n_chunks):
            rows = slice(n * C, (n + 1) * C)
            q_c = qk_s[rows, qs]
            k_c = qk_s[rows, ks]
            sc = (_dot_nt(q_c.astype(BF16), k_c.astype(BF16)) * d_s[h]).astype(BF16)
            q2 = (jnp.concatenate([q_c, q_c], axis=1) * q_dec).astype(BF16)
            o_s[rows, vs] = _dot(sc, v_ref[0, rows, vs]) + _dot(q2, kv_s[n].astype(BF16))

    for h in range(H):
        vs = slice(h * EV, (h + 1) * EV)
        o = o_s[:, vs]
        o = o * lax.rsqrt(jnp.mean(o * o, axis=-1, keepdims=True) + NORM_EPS)
        o_ref[0, :, vs] = (o * g_ref[0, :, vs].astype(F32)).astype(o_ref.dtype)


def _retention(qk, v, gate, log_decay, cos_t, sin_t):
    B, S, _ = qk.shape
    H, EK, EV, C = RET_HEADS, RET_KEY_DIM, RET_VAL_DIM, RET_CHUNK
    bspec = lambda n: pl.BlockSpec((1, S, n), lambda b: (b, 0, 0))
    tab = pl.BlockSpec((S, LANES), lambda b: (0, 0))
    return pl.pallas_call(
        _retention_kernel, grid=(B,),
        in_specs=[pl.BlockSpec(memory_space=pltpu.SMEM), bspec(2 * RET_QK), bspec(RET_V),
                  bspec(RET_V), tab, tab],
        out_specs=bspec(RET_V),
        out_shape=jax.ShapeDtypeStruct((B, S, RET_V), BF16),
        scratch_shapes=[pltpu.VMEM((S, 2 * RET_QK), F32), pltpu.VMEM((H, C, C), F32),
                        pltpu.VMEM((S // C, 2 * EK, EV), F32), pltpu.VMEM((S, RET_V), F32)],
        compiler_params=_cparams("parallel"), name="retention",
    )(log_decay.astype(F32), qk, v, gate, cos_t, sin_t)


def _dft_matrix(S):
    n = 2 * S
    k = jnp.arange(S, dtype=jnp.int32)[:, None]
    t = jnp.arange(S, dtype=jnp.int32)[None, :]
    ang = ((k * t) % n).astype(F32) * (2.0 * math.pi / n)
    cosb = jnp.cos(ang)
    sinb = -jnp.sin(ang)
    nyq = jnp.where(t % 2 == 0, 1.0, -1.0).astype(F32)
    sinb = jnp.where(k == 0, nyq, sinb)
    return jnp.stack([cosb, sinb])


def _hy_filter_kernel(z_ref, w1_ref, b1_ref, f1_ref, w2_ref, b2_ref, f2_ref, w3_ref, dl_ref, o_ref):
    dot = functools.partial(jnp.dot, preferred_element_type=F32, precision=HIGHEST)
    z = z_ref[...]
    hid = jnp.sin(f1_ref[...] * (dot(z, w1_ref[...]) + b1_ref[...]))
    hid = jnp.sin(f2_ref[...] * (dot(hid, w2_ref[...]) + b2_ref[...]))
    o_ref[...] = dot(hid, w3_ref[...]) * jnp.exp(-z[:, 0:1] * dl_ref[...])


def _hy_spectrum_kernel(f_ref, h_ref, o_ref):
    o_ref[0] = jnp.dot(f_ref[0], h_ref[...], preferred_element_type=F32, precision=HIGHEST)


def _hyena_spectrum(S, dft, w1, b1, f1, w2, b2, f2, w3):
    t = jnp.linspace(0.0, 1.0, S, dtype=F32)[:, None]
    bands = (HY_EMB - 1) // 2
    wv = 2.0 * math.pi * jnp.arange(S, dtype=F32) / S
    fr = jnp.linspace(1e-4, bands - 1, bands, dtype=F32)
    ang = wv[:, None] * fr[None, :]
    z = jnp.concatenate([t, jnp.cos(ang), -jnp.sin(ang)], axis=-1)
    z = jnp.pad(z, ((0, 0), (0, LANES - HY_EMB)))
    w1p = jnp.pad(w1.astype(F32), ((0, LANES - HY_EMB), (0, 0)))
    deltas = jnp.abs(jnp.linspace(math.log(HY_TARGET) / HY_SLOW, math.log(HY_TARGET) / HY_FAST,
                                  HY_WIDTH, dtype=F32))
    dl = jnp.tile(deltas, 2)[None, :]
    row = lambda a: a.astype(F32).reshape(1, -1)
    filt = pl.pallas_call(
        _hy_filter_kernel, out_shape=jax.ShapeDtypeStruct((S, 2 * HY_WIDTH), F32),
        compiler_params=pltpu.CompilerParams(vmem_limit_bytes=VMEM_LIMIT), name="hyena_filter",
    )(z, w1p, row(b1), row(f1), w2.astype(F32), row(b2), row(f2), w3.astype(F32), dl)
    tf = HY_FREQ_TILE
    return pl.pallas_call(
        _hy_spectrum_kernel, grid=(2, S // tf),
        in_specs=[pl.BlockSpec((1, tf, S), lambda c, f: (c, f, 0)),
                  pl.BlockSpec((S, 2 * HY_WIDTH), lambda c, f: (0, 0))],
        out_specs=pl.BlockSpec((1, tf, 2 * HY_WIDTH), lambda c, f: (c, f, 0)),
        out_shape=jax.ShapeDtypeStruct((2, S, 2 * HY_WIDTH), F32),
        compiler_params=_cparams("parallel", "parallel"), name="hyena_spectrum",
    )(dft, filt)


def _hyena_kernel(x0_ref, x1_ref, v_ref, bias_ref, f_ref, ft_ref, sp_ref, o_ref, u_s, acc_s):
    f = pl.program_id(1)
    S = v_ref.shape[1]
    W = HY_WIDTH
    tf = f_ref.shape[1]

    @pl.when(f == 0)
    def _():
        u_s[...] = (v_ref[0].astype(F32) * x1_ref[0].astype(F32)).astype(BF16)
        acc_s[...] = jnp.zeros_like(acc_s)

    n = 2 * S
    sp_c = sp_ref[0]
    sp_s = sp_ref[1]
    s_re = sp_c[:, :W] + sp_c[:, W:]
    slot0 = (lax.broadcasted_iota(jnp.int32, (tf, W), 0) + f * tf) == 0
    wk = jnp.where(slot0, 1.0 / n, 2.0 / n)
    ca = s_re * wk
    cb2 = jnp.where(slot0, 0.0, (sp_s[:, :W] - sp_s[:, W:]) * wk)
    cd = jnp.where(slot0, (sp_s[:, :W] + sp_s[:, W:]) * wk, ca)

    u = u_s[...]
    u_re = _dot(f_ref[0], u)
    u_im = _dot(f_ref[1], u)
    y_re = (u_re * ca - u_im * cb2).astype(BF16)
    y_im = (u_re * cb2 + u_im * cd).astype(BF16)
    acc_s[...] += _dot(ft_ref[0], y_re) + _dot(ft_ref[1], y_im)

    @pl.when(f == pl.num_programs(1) - 1)
    def _():
        y = acc_s[...] + u_s[...].astype(F32) * bias_ref[...]
        o_ref[0] = (y * x0_ref[0].astype(F32)).astype(o_ref.dtype)


def _hyena(uc, bias, dft_b, dft_tb, spec):
    B, S, _ = uc.shape
    W = HY_WIDTH
    tf = HY_FREQ_TILE
    full = lambda shp: pl.BlockSpec(shp, lambda b, f: (0,) * len(shp))
    slab = lambda c: pl.BlockSpec((1, S, W), lambda b, f: (b, 0, c))
    return pl.pallas_call(
        _hyena_kernel, grid=(B, S // tf),
        in_specs=[slab(0), slab(1), slab(2), full((1, W)),
                  pl.BlockSpec((2, tf, S), lambda b, f: (0, f, 0)),
                  pl.BlockSpec((2, S, tf), lambda b, f: (0, 0, f)),
                  pl.BlockSpec((2, tf, 2 * W), lambda b, f: (0, f, 0))],
        out_specs=pl.BlockSpec((1, S, W), lambda b, f: (b, 0, 0)),
        out_shape=jax.ShapeDtypeStruct((B, S, W), BF16),
        scratch_shapes=[pltpu.VMEM((S, W), BF16), pltpu.VMEM((S, W), F32)],
        compiler_params=_cparams("parallel", "arbitrary"), name="hyena",
    )(uc, uc, uc, bias.astype(F32).reshape(1, -1), dft_b, dft_tb, spec)


def kernel(x, norm_mix, norm_ffn, final_norm, ab_w_in, dn_conv_w, dn_a_log, dn_dt_bias, dn_norm_w, ab_w_out, cd_w_in, ret_log_decay, hy_conv_w, hy_conv_b, hy_w1, hy_b1, hy_f1, hy_w2, hy_b2, hy_f2, hy_w3, hy_bias, cd_w_out, ffn_w_in, ffn_conv_w, ffn_conv_b, ffn_w_out):
    B, S, D = x.shape
    T = B * S
    x2 = x.reshape(T, D)
    cos_t, sin_t = _rope_tables(S, DIL_HEAD_DIM)

    w = ab_w_in[0]
    W = DN_WIDTH
    n_gate = 4 * DN_HEADS
    w_ba = jnp.pad(w[:, 4 * W:4 * W + n_gate], ((0, 0), (0, LANES - n_gate)))
    nh = DN_WIDTH // LANES
    qkv_acts = ((("silu_l2", DN_HEAD_DIM ** -0.5),) * nh + (("silu_l2", 1.0),) * nh
                + (("silu", 1.0),) * nh)
    qkv, zs, ba, dil = _norm_proj(
        x2, S, norm_mix[0],
        [(w[:, :3 * W].astype(BF16), BF16, (dn_conv_w[0], None), qkv_acts),
         (w[:, 3 * W:4 * W].astype(BF16), BF16, None, (("silu", 1.0),) * nh),
         (w_ba.astype(BF16), F32, None, None),
         (w[:, 4 * W + n_gate:].astype(BF16), BF16, None, None)])
    y_a = _deltanet(qkv.reshape(B, S, -1), zs.reshape(B, S, -1), ba.reshape(B, S, -1),
                    dn_a_log[0], dn_dt_bias[0], dn_norm_w[0])
    y_b = _dilated_attention(dil.reshape(B, S, -1), cos_t, sin_t)
    wo = ab_w_out[0].astype(BF16)
    x2 = _out_proj(x2, y_a.reshape(T, -1), y_b.reshape(T, -1), wo[:W], wo[W:])
    x3 = _conv_ffn(x2, S, norm_ffn[0], ffn_w_in[0], ffn_conv_w[0], ffn_conv_b[0],
                   ffn_w_out[0], final_norm, False)

    w = cd_w_in[0]
    c0, c1, c2 = 2 * RET_QK, 2 * RET_QK + RET_V, 2 * RET_QK + 2 * RET_V
    qk, v, gate_s, uc = _norm_proj(
        x3, S, norm_mix[1],
        [(w[:, :c0].astype(BF16), BF16, None, None),
         (w[:, c0:c1].astype(BF16), BF16, None, None),
         (w[:, c1:c2].astype(BF16), BF16, None, (("silu", 1.0),) * (RET_V // LANES)),
         (w[:, c2:].astype(BF16), BF16, (hy_conv_w[0], hy_conv_b[0]), None)])
    y_c = _retention(qk.reshape(B, S, -1), v.reshape(B, S, -1), gate_s.reshape(B, S, -1),
                     ret_log_decay[0], cos_t, sin_t)
    dft = _dft_matrix(S)
    spec = _hyena_spectrum(S, dft, hy_w1[0], hy_b1[0], hy_f1[0], hy_w2[0], hy_b2[0], hy_f2[0],
                           hy_w3[0])
    dft_b = dft.astype(BF16)
    y_d = _hyena(uc.reshape(B, S, -1), hy_bias[0], dft_b, jnp.swapaxes(dft_b, 1, 2), spec)
    wo = cd_w_out[0].astype(BF16)
    x2 = _out_proj(x3, y_c.reshape(T, -1), y_d.reshape(T, -1), wo[:RET_V], wo[RET_V:])
    out = _conv_ffn(x2, S, norm_ffn[1], ffn_w_in[1], ffn_conv_w[1], ffn_conv_b[1],
                    ffn_w_out[1], final_norm, True)
    return out.reshape(B, S, D)
```

```python
import functools
import math

import numpy as np
import jax
import jax.numpy as jnp
from jax import lax
from jax.experimental import pallas as pl
from jax.experimental.pallas import tpu as pltpu

F32 = jnp.float32
BF16 = jnp.bfloat16
HIGHEST = lax.Precision.HIGHEST

NORM_EPS = 1e-6
ROPE_THETA = 10000.0
NEG_INF = -1e30
LOG2E = 1.4426950408889634

D_MODEL = 1024
DN_HEADS = 4
DN_HEAD_DIM = 128
DN_WIDTH = DN_HEADS * DN_HEAD_DIM
DN_CHUNK = 64
DN_CHAINS = 2 * DN_HEADS
DN_PRE_CHUNKS = 4
DIL_HEADS = 8
DIL_HEAD_DIM = 64
DIL_WIDTH = DIL_HEADS * DIL_HEAD_DIM
DIL_PATTERNS = ((128, 1), (512, 4), (2048, 16))
DIL_QBLK = 128
DIL_UNROLL = 16
RET_HEADS = 4
RET_KEY_DIM = 64
RET_VAL_DIM = 128
RET_QK = RET_HEADS * RET_KEY_DIM
RET_V = RET_HEADS * RET_VAL_DIM
RET_CHUNK = 256
HY_WIDTH = 512
HY_EMB = 33
HY_ORDER = 64
HY_TARGET = 1e-2
HY_FAST = 0.3
HY_SLOW = 1.5
HY_FREQ_TILE = 512
D_FF = 2816
PROJ_CHUNK = 256
FF_ROWS = 1024
FF_SPLIT = 2
FF_CHUNK = 256

LANES = 128
VMEM_LIMIT = 56 * 1024 * 1024


def _cparams(*sem):
    return pltpu.CompilerParams(dimension_semantics=sem, vmem_limit_bytes=VMEM_LIMIT)


def _dot(a, b):
    return jnp.dot(a, b, preferred_element_type=F32)


def _dot_nt(a, b):
    return lax.dot_general(a, b, (((1,), (1,)), ((), ())), preferred_element_type=F32)


def _dot_tn(a, b):
    return lax.dot_general(a, b, (((0,), (0,)), ((), ())), preferred_element_type=F32)


def _silu(x):
    return x * (1.0 / (1.0 + jnp.exp(-x)))


def _rms(x, w):
    return x * lax.rsqrt(jnp.mean(x * x, axis=-1, keepdims=True) + NORM_EPS) * w


SUBLANES = 8


def _conv3_rows(x, w0, w1, w2, edge_prev=0.0, edge_next=0.0):
    n = x.shape[0]
    prev = pltpu.roll(x, 1, axis=0)
    nxt = pltpu.roll(x, n - 1, axis=0)
    y = prev * w0 + x * w1 + nxt * w2
    r8 = lax.broadcasted_iota(jnp.int32, (SUBLANES, x.shape[1]), 0)
    head = y[:SUBLANES] + jnp.where(r8 == 0, (edge_prev - prev[:SUBLANES]) * w0, 0.0)
    tail = y[n - SUBLANES:] + jnp.where(r8 == SUBLANES - 1, (edge_next - nxt[n - SUBLANES:]) * w2, 0.0)
    return jnp.concatenate([head, y[SUBLANES:n - SUBLANES], tail], axis=0)


def _proj_kernel(x_ref, xp_ref, xn_ref, nw_ref, *refs, groups, tiles_per_seq):
    i = pl.program_id(0)
    tm = x_ref.shape[0]
    nw = nw_ref[...]
    h = _rms(x_ref[...], nw)
    hb = h.astype(BF16)
    if any(g[0] for g in groups):
        hx = jnp.concatenate([_rms(xp_ref[...], nw), h, _rms(xn_ref[...], nw)], axis=0).astype(BF16)
        lo, hi = SUBLANES, SUBLANES + tm
        first = (i % tiles_per_seq) == 0
        last = (i % tiles_per_seq) == tiles_per_seq - 1
    n_in = sum(1 + int(g[0]) + int(g[1]) for g in groups)
    in_refs, out_refs = refs[:n_in], refs[n_in:]

    def chunk(group, o_ref, w_ref, cw_ref, cb_ref, cs):
        has_conv, has_bias, acts = group
        w = w_ref[:, cs]
        if has_conv:
            yx = _dot(hx, w)
            y = _conv3_rows(yx[lo:hi], cw_ref[0:1, cs], cw_ref[1:2, cs], cw_ref[2:3, cs],
                            jnp.where(first, 0.0, yx[lo - 1:lo]), jnp.where(last, 0.0, yx[hi:hi + 1]))
            if has_bias:
                y = y + cb_ref[:, cs]
        else:
            y = _dot(hb, w)
        for t0 in range(cs.start, cs.stop, LANES):
            yt = y[:, t0 - cs.start:t0 - cs.start + LANES]
            act = acts[t0 // LANES]
            if act is not None:
                yt = _silu(yt)
                if act[0] == "silu_l2":
                    yt = yt * (lax.rsqrt(jnp.sum(yt * yt, axis=-1, keepdims=True) + 1e-6) * act[1])
            o_ref[:, t0:t0 + LANES] = yt.astype(o_ref.dtype)

    heavy, light = [], []
    pos = 0
    for group, o_ref in zip(groups, out_refs):
        has_conv, has_bias, acts = group
        w_ref = in_refs[pos]
        cw_ref = in_refs[pos + 1] if has_conv else None
        cb_ref = in_refs[pos + 1 + int(has_conv)] if has_bias else None
        pos += 1 + int(has_conv) + int(has_bias)
        n_cols = w_ref.shape[1]
        for c0 in range(0, n_cols, PROJ_CHUNK):
            item = (group, o_ref, w_ref, cw_ref, cb_ref, slice(c0, min(c0 + PROJ_CHUNK, n_cols)))
            (heavy if has_conv or any(a is not None for a in acts) else light).append(item)
    while heavy or light:
        if heavy:
            chunk(*heavy.pop(0))
        if light:
            chunk(*light.pop(0))


def _norm_proj(x2d, seq_len, norm_w, outs, tm=512):
    T, D = x2d.shape
    nblk8 = T // 8
    const = lambda shp: pl.BlockSpec(shp, lambda i: (0, 0))
    in_specs = [pl.BlockSpec((tm, D), lambda i: (i, 0)),
                pl.BlockSpec((8, D), lambda i: (jnp.maximum(i * (tm // 8) - 1, 0), 0)),
                pl.BlockSpec((8, D), lambda i: (jnp.minimum((i + 1) * (tm // 8), nblk8 - 1), 0)),
                const((1, D))]
    args = [x2d, x2d, x2d, norm_w.reshape(1, D)]
    groups, out_specs, out_shape = [], [], []
    for w, dtype, conv, acts in outs:
        n = w.shape[1]
        args.append(w)
        in_specs.append(const(w.shape))
        if conv is not None:
            cw, cb = conv
            args.append(cw.astype(F32))
            in_specs.append(const((3, n)))
            if cb is not None:
                args.append(cb.astype(F32).reshape(1, n))
                in_specs.append(const((1, n)))
        acts = acts if acts is not None else (None,) * (n // LANES)
        groups.append((conv is not None, conv is not None and conv[1] is not None, tuple(acts)))
        out_specs.append(pl.BlockSpec((tm, n), lambda i: (i, 0)))
        out_shape.append(jax.ShapeDtypeStruct((T, n), dtype))
    return pl.pallas_call(
        functools.partial(_proj_kernel, groups=tuple(groups), tiles_per_seq=seq_len // tm),
        grid=(T // tm,), in_specs=in_specs, out_specs=out_specs,
        out_shape=out_shape, compiler_params=_cparams("parallel"), name="norm_proj",
    )(*args)


def _outproj_kernel(x_ref, ya_ref, yb_ref, wa_ref, wb_ref, o_ref):
    o_ref[...] = x_ref[...] + _dot(ya_ref[...], wa_ref[...]) + _dot(yb_ref[...], wb_ref[...])


def _out_proj(x2d, ya, yb, wa, wb, tm=1024):
    T, D = x2d.shape
    row = lambda i: (i, 0)
    full = lambda i: (0, 0)
    return pl.pallas_call(
        _outproj_kernel, grid=(T // tm,),
        in_specs=[pl.BlockSpec((tm, D), row), pl.BlockSpec((tm, ya.shape[1]), row),
                  pl.BlockSpec((tm, yb.shape[1]), row), pl.BlockSpec(wa.shape, full),
                  pl.BlockSpec(wb.shape, full)],
        out_specs=pl.BlockSpec((tm, D), row),
        out_shape=jax.ShapeDtypeStruct((T, D), F32),
        compiler_params=_cparams("parallel"), name="out_proj",
    )(x2d, ya, yb, wa, wb)


def _ffn_kernel(x_ref, xp_ref, xn_ref, nw_ref, wg_ref, wu_ref, cw_ref, cb_ref, wo_ref, fw_ref,
                o_ref, h_s, act_s, *, final_norm, tiles_per_seq):
    i = pl.program_id(0)
    j = pl.program_id(1)
    tm = x_ref.shape[0]
    tfo = wg_ref.shape[1]
    lo, hi = SUBLANES, SUBLANES + tm

    @pl.when(j == 0)
    def _():
        nw = nw_ref[...]
        h_s[...] = jnp.concatenate([_rms(xp_ref[...], nw), _rms(x_ref[...], nw),
                                    _rms(xn_ref[...], nw)], axis=0).astype(BF16)

    first = (i % tiles_per_seq) == 0
    last = (i % tiles_per_seq) == tiles_per_seq - 1
    hx = h_s[...]
    for c0 in range(0, tfo, FF_CHUNK):
        cs = slice(c0, min(c0 + FF_CHUNK, tfo))
        gate = _dot(hx, wg_ref[:, cs])
        up = _dot(hx, wu_ref[:, cs])
        g_prev = jnp.where(first, 0.0, gate[lo - 1:lo])
        g_next = jnp.where(last, 0.0, gate[hi:hi + 1])
        g = _conv3_rows(gate[lo:hi], cw_ref[0:1, cs], cw_ref[1:2, cs], cw_ref[2:3, cs],
                        g_prev, g_next) + cb_ref[:, cs]
        act_s[j, :, cs] = (_silu(g) * up[lo:hi]).astype(BF16)

    @pl.when(j == pl.num_programs(1) - 1)
    def _():
        y = x_ref[...]
        for jj in range(act_s.shape[0]):
            y = y + _dot(act_s[jj], wo_ref[jj * tfo:(jj + 1) * tfo, :])
        o_ref[...] = _rms(y, fw_ref[...]) if final_norm else y


def _conv_ffn(x2d, seq_len, norm_w, w_in, conv_w, conv_b, w_out, final_w, final_norm):
    T, D = x2d.shape
    tm = FF_ROWS
    nj = FF_SPLIT
    tfo = D_FF // nj
    nblk8 = T // 8
    wg = w_in[:, :D_FF].astype(BF16)
    wu = w_in[:, D_FF:].astype(BF16)
    wo = w_out.astype(BF16)
    return pl.pallas_call(
        functools.partial(_ffn_kernel, final_norm=final_norm, tiles_per_seq=seq_len // tm),
        grid=(T // tm, nj),
        in_specs=[pl.BlockSpec((tm, D), lambda i, j: (i, 0)),
                  pl.BlockSpec((8, D), lambda i, j: (jnp.maximum(i * (tm // 8) - 1, 0), 0)),
                  pl.BlockSpec((8, D), lambda i, j: (jnp.minimum((i + 1) * (tm // 8), nblk8 - 1), 0)),
                  pl.BlockSpec((1, D), lambda i, j: (0, 0)),
                  pl.BlockSpec((D, tfo), lambda i, j: (0, j)),
                  pl.BlockSpec((D, tfo), lambda i, j: (0, j)),
                  pl.BlockSpec((3, tfo), lambda i, j: (0, j)),
                  pl.BlockSpec((1, tfo), lambda i, j: (0, j)),
                  pl.BlockSpec((D_FF, D), lambda i, j: (0, 0)),
                  pl.BlockSpec((1, D), lambda i, j: (0, 0))],
        out_specs=pl.BlockSpec((tm, D), lambda i, j: (i, 0)),
        out_shape=jax.ShapeDtypeStruct((T, D), F32),
        scratch_shapes=[pltpu.VMEM((tm + 2 * SUBLANES, D), BF16), pltpu.VMEM((nj, tm, tfo), BF16)],
        compiler_params=_cparams("parallel", "arbitrary"), name="conv_ffn",
    )(x2d, x2d, x2d, norm_w.reshape(1, D), wg, wu, conv_w, conv_b.reshape(1, D_FF), wo,
      final_w.reshape(1, D))


def _deltanet_kernel(q_ref, k_ref, v_ref, zs_ref, ba_ref, gp_ref, nw_ref, o_ref,
                     pk_s, o_s, u_s, w_s, in_s, st_s, et_s):
    S = q_ref.shape[1]
    W, E, C, H = DN_WIDTH, DN_HEAD_DIM, DN_CHUNK, DN_HEADS
    n_chunks = S // C
    row = lax.broadcasted_iota(jnp.int32, (S, LANES), 0)
    lane = lax.broadcasted_iota(jnp.int32, (S, LANES), 1)
    q_s, k_s, v_s = q_ref.at[0], k_ref.at[0], v_ref.at[0]

    ba = ba_ref[0]
    beta = 1.0 / (1.0 + jnp.exp(-ba))
    xg = ba + gp_ref[1:2, :]
    g = gp_ref[0:1, :] * (jnp.maximum(xg, 0.0) + jnp.log1p(jnp.exp(-jnp.abs(xg))))
    pos = row & (C - 1)
    pre = g
    suf = g
    sh = 1
    while sh < C:
        pre = pre + jnp.where(pos >= sh, pltpu.roll(pre, sh, axis=0), 0.0)
        suf = suf + jnp.where(pos < C - sh, pltpu.roll(suf, S - sh, axis=0), 0.0)
        sh *= 2
    fwd = lane < 8 + H
    gc = jnp.where(fwd, pre, suf)
    glc = jnp.where(fwd, suf, pre) - g
    pk = jnp.where(lane < 8, beta,
                   jnp.where(lane < 16, gc,
                             jnp.where(lane < 24, pltpu.roll(glc, 8, axis=1),
                                       pltpu.roll(gc + glc, 16, axis=1))))
    pk_s[...] = pk

    st_s[...] = jnp.zeros_like(st_s)
    o_s[...] = jnp.zeros_like(o_s)

    ri = lax.broadcasted_iota(jnp.int32, (C, LANES), 0)
    ci = lax.broadcasted_iota(jnp.int32, (C, LANES), 1)
    lo = ci < C
    eye_hi = (ci == ri + C).astype(F32)
    zeros_b = jnp.zeros((C, E), BF16)

    heads = [slice(h * E, (h + 1) * E) for h in range(H)]
    chains = [(d, h) for d in range(2) for h in range(H)]


    def pre_body(i, carry):
        units = [(cc, d, h) for cc in range(DN_PRE_CHUNKS) for d, h in chains]
        rows, pkc, tr, k_c, v_c, s2 = [], [], [], [], [], []
        for cc in range(DN_PRE_CHUNKS):
            rw = pl.ds(pl.multiple_of((i * DN_PRE_CHUNKS + cc) * C, C), C)
            rows.append(rw)
            pk_c = pk_s[rw, :]
            pkc.append(pk_c)
            tr.append(jnp.transpose(jnp.concatenate([pk_c, pk_c], axis=0)))
            et_s[i * DN_PRE_CHUNKS + cc] = jnp.exp(tr[cc][3 * DN_CHAINS:4 * DN_CHAINS, :])
            k_c.append([k_s[rw, hs] for hs in heads])
            v_c.append([v_s[rw, hs] for hs in heads])
            s2.append([_dot_nt(jnp.concatenate([q_s[rw, hs], k], axis=0),
                               jnp.concatenate([k, zeros_b], axis=0))
                       for hs, k in zip(heads, k_c[cc])])
        z, b_row, bg_row = [], [], []
        for cc, d, h in units:
            c = d * H + h
            causal = ((ri >= ci) if d == 0 else (ri <= ci)) & lo
            strict = ((ri > ci) if d == 0 else (ri < ci)) & lo
            beta_col = pkc[cc][:, c:c + 1]
            gc_col = pkc[cc][:, 8 + c:9 + c]
            beta_row = tr[cc][c:c + 1, :]
            gc_row = tr[cc][8 + c:9 + c, :]
            dmat = jnp.where(causal, jnp.exp(jnp.where(causal, gc_col - gc_row, 0.0)), 0.0)
            in_s[c, rows[cc], :] = (s2[cc][h][:C] * dmat).astype(BF16)
            z.append(eye_hi - jnp.where(strict, s2[cc][h][C:] * beta_col * dmat, 0.0))
            b_row.append(beta_row)
            bg_row.append(beta_row * jnp.exp(gc_row))
        for it in range(6):
            zb = [x.astype(BF16) for x in z]
            z = [_dot(b, jnp.concatenate([b, zeros_b], axis=0)) + jnp.where(lo, 0.0, x)
                 for b, x in zip(zb, z)]
        us = [_dot((x * br).astype(BF16), jnp.concatenate([zeros_b, v_c[cc][h]], axis=0))
              for (cc, d, h), x, br in zip(units, z, b_row)]
        ws = [_dot((x * br).astype(BF16), jnp.concatenate([zeros_b, k_c[cc][h]], axis=0))
              for (cc, d, h), x, br in zip(units, z, bg_row)]
        for (cc, d, h), u, w in zip(units, us, ws):
            u_s[d, rows[cc], heads[h]] = u.astype(BF16)
            w_s[d, rows[cc], heads[h]] = w.astype(BF16)
        return carry

    lax.fori_loop(0, n_chunks // DN_PRE_CHUNKS, pre_body, 0)

    def rec_body(n, carry):
        rows, ex, et = [], [], []
        for d in range(2):
            nd = n if d == 0 else n_chunks - 1 - n
            rows.append(pl.ds(pl.multiple_of(nd * C, C), C))
            ex.append(jnp.exp(pk_s[rows[d], :]))
            et.append(et_s[nd])
        st = [st_s[d * H + h] for d, h in chains]
        wq = [_dot(jnp.concatenate([w_s[d, rows[d], heads[h]], q_s[rows[d], heads[h]]], axis=0),
                   s.astype(BF16)) for (d, h), s in zip(chains, st)]
        for (d, h), s, x in zip(chains, st, wq):
            c = d * H + h
            eg_col = ex[d][:, 8 + c:9 + c]
            eglc_col = ex[d][:, 16 + c:17 + c]
            v_new = u_s[d, rows[d], heads[h]].astype(F32) - x[:C]
            o = eg_col * x[C:] + _dot(in_s[c, rows[d], :],
                                      jnp.concatenate([v_new.astype(BF16), zeros_b], axis=0))
            upd = _dot_tn(k_s[rows[d], heads[h]], (eglc_col * v_new).astype(BF16))
            st_s[c] = et[d][c:c + 1, :] * s + upd
            o_s[rows[d], heads[h]] += o
        return carry

    lax.fori_loop(0, n_chunks, rec_body, 0)

    nw = nw_ref[...]
    for h in range(H):
        hs = slice(h * E, (h + 1) * E)
        o = o_s[:, hs]
        o = o * lax.rsqrt(jnp.mean(o * o, axis=-1, keepdims=True) + NORM_EPS) * nw
        o_ref[0, :, hs] = (o * zs_ref[0, :, hs].astype(F32)).astype(o_ref.dtype)


def _deltanet(qkv, zs, ba, a_log, dt_bias, norm_w):
    B, S, _ = qkv.shape
    W, E = DN_WIDTH, DN_HEAD_DIM
    gp = jnp.zeros((2, LANES), F32)
    gp = gp.at[0, 8:16].set(-jnp.exp(a_log.astype(F32)).reshape(-1))
    gp = gp.at[1, 8:16].set(dt_bias.astype(F32).reshape(-1))
    bspec = lambda n: pl.BlockSpec((1, S, n), lambda b: (b, 0, 0))
    full = lambda shp: pl.BlockSpec(shp, lambda b: (0,) * len(shp))
    return pl.pallas_call(
        _deltanet_kernel, grid=(B,),
        in_specs=[pl.BlockSpec((1, S, W), lambda b: (b, 0, 0)),
                  pl.BlockSpec((1, S, W), lambda b: (b, 0, 1)),
                  pl.BlockSpec((1, S, W), lambda b: (b, 0, 2)),
                  bspec(W), bspec(LANES), full((2, LANES)), full((1, E))],
        out_specs=bspec(W),
        out_shape=jax.ShapeDtypeStruct((B, S, W), BF16),
        scratch_shapes=[pltpu.VMEM((S, LANES), F32), pltpu.VMEM((S, W), F32),
                        pltpu.VMEM((2, S, W), BF16), pltpu.VMEM((2, S, W), BF16),
                        pltpu.VMEM((DN_CHAINS, S, LANES), BF16),
                        pltpu.VMEM((DN_CHAINS, E, E), F32),
                        pltpu.VMEM((S // DN_CHUNK, DN_CHAINS, LANES), F32)],
        compiler_params=_cparams("parallel"), name="deltanet",
    )(qkv, qkv, qkv, zs, ba, gp, norm_w.reshape(1, E).astype(F32))


def _rope_tables(S, E):
    inv = ROPE_THETA ** (-jnp.arange(0, E, 2, dtype=F32) / E)
    ang = jnp.arange(S, dtype=F32)[:, None] * inv[None, :]
    cos, sin = jnp.cos(ang), jnp.sin(ang)
    cos_t = jnp.tile(jnp.concatenate([cos, cos], axis=-1), (1, LANES // E))
    sin_t = jnp.tile(jnp.concatenate([-sin, sin], axis=-1), (1, LANES // E))
    return cos_t, sin_t


def _rope_tile(x, cos_t, sin_t, lane):
    half = DIL_HEAD_DIM // 2
    partner = jnp.where((lane & (DIL_HEAD_DIM - 1)) < half,
                        pltpu.roll(x, LANES - half, axis=1), pltpu.roll(x, half, axis=1))
    return x * cos_t + partner * sin_t


def _dilated_kernel(q_ref, k_ref, v_ref, cos_ref, sin_ref, o_ref,
                    q_s, k_s, v_s, num_s, den_s, max_s, bias_s):
    S = q_ref.shape[1]
    E = DIL_HEAD_DIM
    QB = DIL_QBLK
    lane = lax.broadcasted_iota(jnp.int32, (S, LANES), 1)
    cos_t = cos_ref[...]
    sin_t = sin_ref[...]
    q_s[...] = _rope_tile(q_ref[0].astype(F32), cos_t, sin_t, lane) * (E ** -0.5 * LOG2E)
    k_s[...] = _rope_tile(k_ref[0].astype(F32), cos_t, sin_t, lane)
    v_s[...] = v_ref[0].astype(F32)
    head0 = lax.broadcasted_iota(jnp.int32, (QB, LANES), 1) < E

    for p, (window, dil) in enumerate(DIL_PATTERNS):
        half = window // (2 * dil)
        L = S // dil
        nb = L // QB
        KW = min(QB + 2 * half, L)
        relm = (lax.broadcasted_iota(jnp.int32, (QB, KW), 1)
                - lax.broadcasted_iota(jnp.int32, (QB, KW), 0))
        for case, off in enumerate((0, -half, QB - KW)):
            bias_s[case, :, :KW] = jnp.where(jnp.abs(relm + off) <= half, 0.0, NEG_INF)
        ones_b = jnp.ones((KW, LANES), BF16)

        def block(i, carry, dil=dil, half=half, L=L, nb=nb, KW=KW, p=p, ones_b=ones_b):
            qrows, biases, vas, scores = [], [], [], []
            for uu in range(DIL_UNROLL):
                it = i * DIL_UNROLL + uu
                r = it // nb
                n = it % nb
                q0 = n * QB
                k0 = jnp.clip(q0 - half, 0, L - KW)
                if dil == 1:
                    qr = pl.ds(pl.multiple_of(q0, QB), QB)
                    kr = pl.ds(pl.multiple_of(k0, half), KW)
                else:
                    qr = pl.ds(r + dil * q0, QB, stride=dil)
                    kr = pl.ds(r + dil * k0, KW, stride=dil)
                qrows.append(qr)
                case = jnp.where(n == 0, 0, jnp.where(n == nb - 1, 2, 1))
                biases.append(bias_s[case, :, :KW])
                qf = q_s[qr, :]
                kb = k_s[kr, :].astype(BF16)
                vas.append(jnp.concatenate([v_s[kr, :].astype(BF16), ones_b], axis=1))
                for hh in range(2):
                    qh = jnp.where(head0 if hh == 0 else ~head0, qf, 0.0).astype(BF16)
                    scores.append(_dot_nt(qh, kb))
            ms, es = [], []
            for j, s in enumerate(scores):
                s = s + biases[j // 2]
                m = jnp.max(s, axis=-1, keepdims=True)
                ms.append(m)
                es.append(jnp.exp2(s - m).astype(BF16))
            oa = [_dot(e, vas[j // 2]) for j, e in enumerate(es)]
            for uu in range(DIL_UNROLL):
                a0, a1 = oa[2 * uu], oa[2 * uu + 1]
                num_s[p, qrows[uu], :] = jnp.where(head0, a0[:, :LANES], a1[:, :LANES])
                den_s[p, qrows[uu], :] = jnp.where(head0, a0[:, LANES:], a1[:, LANES:])
                max_s[p, qrows[uu], :] = jnp.where(head0, ms[2 * uu], ms[2 * uu + 1])
            return carry

        lax.fori_loop(0, dil * nb // DIL_UNROLL, block, 0)

    mx = jnp.maximum(jnp.maximum(max_s[0], max_s[1]), max_s[2])
    num = jnp.zeros((S, LANES), F32)
    den = jnp.zeros((S, LANES), F32)
    for p in range(len(DIL_PATTERNS)):
        w = jnp.exp2(max_s[p] - mx)
        num = num + w * num_s[p]
        den = den + w * den_s[p]
    o_ref[0] = (num * (1.0 / den)).astype(o_ref.dtype)


def _dilated_attention(qkv, cos_t, sin_t):
    B, S, _ = qkv.shape
    nt = DIL_WIDTH // LANES
    tile = lambda off: pl.BlockSpec((1, S, LANES), lambda b, t: (b, 0, t + off))
    tab = pl.BlockSpec((S, LANES), lambda b, t: (0, 0))
    return pl.pallas_call(
        _dilated_kernel, grid=(B, nt),
        in_specs=[tile(0), tile(nt), tile(2 * nt), tab, tab],
        out_specs=tile(0),
        out_shape=jax.ShapeDtypeStruct((B, S, DIL_WIDTH), BF16),
        scratch_shapes=[pltpu.VMEM((S, LANES), F32)] * 3
        + [pltpu.VMEM((len(DIL_PATTERNS), S, LANES), F32)] * 3
        + [pltpu.VMEM((3, DIL_QBLK, 2 * LANES), F32)],
        compiler_params=_cparams("parallel", "parallel"), name="dilated_attention",
    )(qkv, qkv, qkv, cos_t, sin_t)


def _retention_kernel(lg_ref, qk_ref, v_ref, g_ref, cos_ref, sin_ref, o_ref,
                      qk_s, d_s, kv_s, o_s):
    S = qk_ref.shape[1]
    H, EK, EV, C = RET_HEADS, RET_KEY_DIM, RET_VAL_DIM, RET_CHUNK
    n_chunks = S // C
    lane = lax.broadcasted_iota(jnp.int32, (S, LANES), 1)
    cos_t = cos_ref[...]
    sin_t = sin_ref[...]
    for t in range(2 * RET_QK // LANES):
        sl = slice(t * LANES, (t + 1) * LANES)
        y = _rope_tile(qk_ref[0, :, sl].astype(F32), cos_t, sin_t, lane)
        qk_s[:, sl] = y * (EK ** -0.5) if t < RET_QK // LANES else y

    rel = (lax.broadcasted_iota(jnp.int32, (C, C), 0)
           - lax.broadcasted_iota(jnp.int32, (C, C), 1)).astype(F32)
    idx = lax.broadcasted_iota(jnp.int32, (C, EK), 0).astype(F32)
    for h in range(H):
        lgf = lg_ref[0, h]
        lgb = lg_ref[1, h]
        d_s[h] = jnp.exp(lgf * jnp.maximum(rel, 0.0) + lgb * jnp.maximum(-rel, 0.0))
        k_dec = jnp.concatenate([jnp.exp(lgf * (C - 1 - idx)), jnp.exp(lgb * idx)], axis=1)
        q_dec = jnp.concatenate([jnp.exp(lgf * (idx + 1)), jnp.exp(lgb * (C - idx))], axis=1)
        qs = slice(h * EK, (h + 1) * EK)
        ks = slice(RET_QK + h * EK, RET_QK + (h + 1) * EK)
        vs = slice(h * EV, (h + 1) * EV)
        for n in range(n_chunks):
            rows = slice(n * C, (n + 1) * C)
            k_c = qk_s[rows, ks]
            k2 = (jnp.concatenate([k_c, k_c], axis=1) * k_dec).astype(BF16)
            kv_s[n] = _dot_tn(k2, v_ref[0, rows, vs])
        fdec = jnp.exp(lgf * C)
        bdec = jnp.exp(lgb * C)
        st = jnp.zeros((EK, EV), F32)
        for n in range(n_chunks):
            inc = kv_s[n, :EK, :]
            kv_s[n, :EK, :] = st
            st = st * fdec + inc
        st = jnp.zeros((EK, EV), F32)
        for n in range(n_chunks - 1, -1, -1):
            inc = kv_s[n, EK:, :]
            kv_s[n, EK:, :] = st
            st = st * bdec + inc
        for n in range(n_chunks):
            rows = slice(n * C, (n + 1) * C)
            q_c = qk_s[rows, qs]
            k_c = qk_s[rows, ks]
            sc = (_dot_nt(q_c.astype(BF16), k_c.astype(BF16)) * d_s[h]).astype(BF16)
            q2 = (jnp.concatenate([q_c, q_c], axis=1) * q_dec).astype(BF16)
            o_s[rows, vs] = _dot(sc, v_ref[0, rows, vs]) + _dot(q2, kv_s[n].astype(BF16))

    for h in range(H):
        vs = slice(h * EV, (h + 1) * EV)
        o = o_s[:, vs]
        o = o * lax.rsqrt(jnp.mean(o * o, axis=-1, keepdims=True) + NORM_EPS)
        o_ref[0, :, vs] = (o * g_ref[0, :, vs].astype(F32)).astype(o_ref.dtype)


def _retention(qk, v, gate, log_decay, cos_t, sin_t):
    B, S, _ = qk.shape
    H, EK, EV, C = RET_HEADS, RET_KEY_DIM, RET_VAL_DIM, RET_CHUNK
    bspec = lambda n: pl.BlockSpec((1, S, n), lambda b: (b, 0, 0))
    tab = pl.BlockSpec((S, LANES), lambda b: (0, 0))
    return pl.pallas_call(
        _retention_kernel, grid=(B,),
        in_specs=[pl.BlockSpec(memory_space=pltpu.SMEM), bspec(2 * RET_QK), bspec(RET_V),
                  bspec(RET_V), tab, tab],
        out_specs=bspec(RET_V),
        out_shape=jax.ShapeDtypeStruct((B, S, RET_V), BF16),
        scratch_shapes=[pltpu.VMEM((S, 2 * RET_QK), F32), pltpu.VMEM((H, C, C), F32),
                        pltpu.VMEM((S // C, 2 * EK, EV), F32), pltpu.VMEM((S, RET_V), F32)],
        compiler_params=_cparams("parallel"), name="retention",
    )(log_decay.astype(F32), qk, v, gate, cos_t, sin_t)


def _dft_matrix(S):
    n = 2 * S
    k = jnp.arange(S, dtype=jnp.int32)[:, None]
    t = jnp.arange(S, dtype=jnp.int32)[None, :]
    ang = ((k * t) % n).astype(F32) * (2.0 * math.pi / n)
    cosb = jnp.cos(ang)
    sinb = -jnp.sin(ang)
    nyq = jnp.where(t % 2 == 0, 1.0, -1.0).astype(F32)
    sinb = jnp.where(k == 0, nyq, sinb)
    return jnp.stack([cosb, sinb])


def _hy_filter_kernel(z_ref, w1_ref, b1_ref, f1_ref, w2_ref, b2_ref, f2_ref, w3_ref, dl_ref, o_ref):
    dot = functools.partial(jnp.dot, preferred_element_type=F32, precision=HIGHEST)
    z = z_ref[...]
    hid = jnp.sin(f1_ref[...] * (dot(z, w1_ref[...]) + b1_ref[...]))
    hid = jnp.sin(f2_ref[...] * (dot(hid, w2_ref[...]) + b2_ref[...]))
    o_ref[...] = dot(hid, w3_ref[...]) * jnp.exp(-z[:, 0:1] * dl_ref[...])


def _hy_spectrum_kernel(f_ref, h_ref, o_ref):
    o_ref[0] = jnp.dot(f_ref[0], h_ref[...], preferred_element_type=F32, precision=HIGHEST)


def _hyena_spectrum(S, dft, w1, b1, f1, w2, b2, f2, w3):
    t = jnp.linspace(0.0, 1.0, S, dtype=F32)[:, None]
    bands = (HY_EMB - 1) // 2
    wv = 2.0 * math.pi * jnp.arange(S, dtype=F32) / S
    fr = jnp.linspace(1e-4, bands - 1, bands, dtype=F32)
    ang = wv[:, None] * fr[None, :]
    z = jnp.concatenate([t, jnp.cos(ang), -jnp.sin(ang)], axis=-1)
    z = jnp.pad(z, ((0, 0), (0, LANES - HY_EMB)))
    w1p = jnp.pad(w1.astype(F32), ((0, LANES - HY_EMB), (0, 0)))
    deltas = jnp.abs(jnp.linspace(math.log(HY_TARGET) / HY_SLOW, math.log(HY_TARGET) / HY_FAST,
                                  HY_WIDTH, dtype=F32))
    dl = jnp.tile(deltas, 2)[None, :]
    row = lambda a: a.astype(F32).reshape(1, -1)
    filt = pl.pallas_call(
        _hy_filter_kernel, out_shape=jax.ShapeDtypeStruct((S, 2 * HY_WIDTH), F32),
        compiler_params=pltpu.CompilerParams(vmem_limit_bytes=VMEM_LIMIT), name="hyena_filter",
    )(z, w1p, row(b1), row(f1), w2.astype(F32), row(b2), row(f2), w3.astype(F32), dl)
    tf = HY_FREQ_TILE
    return pl.pallas_call(
        _hy_spectrum_kernel, grid=(2, S // tf),
        in_specs=[pl.BlockSpec((1, tf, S), lambda c, f: (c, f, 0)),
                  pl.BlockSpec((S, 2 * HY_WIDTH), lambda c, f: (0, 0))],
        out_specs=pl.BlockSpec((1, tf, 2 * HY_WIDTH), lambda c, f: (c, f, 0)),
        out_shape=jax.ShapeDtypeStruct((2, S, 2 * HY_WIDTH), F32),
        compiler_params=_cparams("parallel", "parallel"), name="hyena_spectrum",
    )(dft, filt)


def _hyena_kernel(x0_ref, x1_ref, v_ref, bias_ref, f_ref, ft_ref, sp_ref, o_ref, u_s, acc_s):
    f = pl.program_id(1)
    S = v_ref.shape[1]
    W = HY_WIDTH
    tf = f_ref.shape[1]

    @pl.when(f == 0)
    def _():
        u_s[...] = (v_ref[0].astype(F32) * x1_ref[0].astype(F32)).astype(BF16)
        acc_s[...] = jnp.zeros_like(acc_s)

    n = 2 * S
    sp_c = sp_ref[0]
    sp_s = sp_ref[1]
    s_re = sp_c[:, :W] + sp_c[:, W:]
    slot0 = (lax.broadcasted_iota(jnp.int32, (tf, W), 0) + f * tf) == 0
    wk = jnp.where(slot0, 1.0 / n, 2.0 / n)
    ca = s_re * wk
    cb2 = jnp.where(slot0, 0.0, (sp_s[:, :W] - sp_s[:, W:]) * wk)
    cd = jnp.where(slot0, (sp_s[:, :W] + sp_s[:, W:]) * wk, ca)

    u = u_s[...]
    u_re = _dot(f_ref[0], u)
    u_im = _dot(f_ref[1], u)
    y_re = (u_re * ca - u_im * cb2).astype(BF16)
    y_im = (u_re * cb2 + u_im * cd).astype(BF16)
    acc_s[...] += _dot(ft_ref[0], y_re) + _dot(ft_ref[1], y_im)

    @pl.when(f == pl.num_programs(1) - 1)
    def _():
        y = acc_s[...] + u_s[...].astype(F32) * bias_ref[...]
        o_ref[0] = (y * x0_ref[0].astype(F32)).astype(o_ref.dtype)


def _hyena(uc, bias, dft_b, dft_tb, spec):
    B, S, _ = uc.shape
    W = HY_WIDTH
    tf = HY_FREQ_TILE
    full = lambda shp: pl.BlockSpec(shp, lambda b, f: (0,) * len(shp))
    slab = lambda c: pl.BlockSpec((1, S, W), lambda b, f: (b, 0, c))
    return pl.pallas_call(
        _hyena_kernel, grid=(B, S // tf),
        in_specs=[slab(0), slab(1), slab(2), full((1, W)),
                  pl.BlockSpec((2, tf, S), lambda b, f: (0, f, 0)),
                  pl.BlockSpec((2, S, tf), lambda b, f: (0, 0, f)),
                  pl.BlockSpec((2, tf, 2 * W), lambda b, f: (0, f, 0))],
        out_specs=pl.BlockSpec((1, S, W), lambda b, f: (b, 0, 0)),
        out_shape=jax.ShapeDtypeStruct((B, S, W), BF16),
        scratch_shapes=[pltpu.VMEM((S, W), BF16), pltpu.VMEM((S, W), F32)],
        compiler_params=_cparams("parallel", "arbitrary"), name="hyena",
    )(uc, uc, uc, bias.astype(F32).reshape(1, -1), dft_b, dft_tb, spec)


def kernel(x, norm_mix, norm_ffn, final_norm, ab_w_in, dn_conv_w, dn_a_log, dn_dt_bias, dn_norm_w, ab_w_out, cd_w_in, ret_log_decay, hy_conv_w, hy_conv_b, hy_w1, hy_b1, hy_f1, hy_w2, hy_b2, hy_f2, hy_w3, hy_bias, cd_w_out, ffn_w_in, ffn_conv_w, ffn_conv_b, ffn_w_out):
    B, S, D = x.shape
    T = B * S
    x2 = x.reshape(T, D)
    cos_t, sin_t = _rope_tables(S, DIL_HEAD_DIM)

    w = ab_w_in[0]
    W = DN_WIDTH
    n_gate = 4 * DN_HEADS
    w_ba = jnp.pad(w[:, 4 * W:4 * W + n_gate], ((0, 0), (0, LANES - n_gate)))
    nh = DN_WIDTH // LANES
    qkv_acts = ((("silu_l2", DN_HEAD_DIM ** -0.5),) * nh + (("silu_l2", 1.0),) * nh
                + (("silu", 1.0),) * nh)
    qkv, zs, ba, dil = _norm_proj(
        x2, S, norm_mix[0],
        [(w[:, :3 * W].astype(BF16), BF16, (dn_conv_w[0], None), qkv_acts),
         (w[:, 3 * W:4 * W].astype(BF16), BF16, None, (("silu", 1.0),) * nh),
         (w_ba.astype(BF16), F32, None, None),
         (w[:, 4 * W + n_gate:].astype(BF16), BF16, None, None)])
    y_a = _deltanet(qkv.reshape(B, S, -1), zs.reshape(B, S, -1), ba.reshape(B, S, -1),
                    dn_a_log[0], dn_dt_bias[0], dn_norm_w[0])
    y_b = _dilated_attention(dil.reshape(B, S, -1), cos_t, sin_t)
    wo = ab_w_out[0].astype(BF16)
    x2 = _out_proj(x2, y_a.reshape(T, -1), y_b.reshape(T, -1), wo[:W], wo[W:])
    x3 = _conv_ffn(x2, S, norm_ffn[0], ffn_w_in[0], ffn_conv_w[0], ffn_conv_b[0],
                   ffn_w_out[0], final_norm, False)

    w = cd_w_in[0]
    c0, c1, c2 = 2 * RET_QK, 2 * RET_QK + RET_V, 2 * RET_QK + 2 * RET_V
    qk, v, gate_s, uc = _norm_proj(
        x3, S, norm_mix[1],
        [(w[:, :c0].astype(BF16), BF16, None, None),
         (w[:, c0:c1].astype(BF16), BF16, None, None),
         (w[:, c1:c2].astype(BF16), BF16, None, (("silu", 1.0),) * (RET_V // LANES)),
         (w[:, c2:].astype(BF16), BF16, (hy_conv_w[0], hy_conv_b[0]), None)])
    y_c = _retention(qk.reshape(B, S, -1), v.reshape(B, S, -1), gate_s.reshape(B, S, -1),
                     ret_log_decay[0], cos_t, sin_t)
    dft = _dft_matrix(S)
    spec = _hyena_spectrum(S, dft, hy_w1[0], hy_b1[0], hy_f1[0], hy_w2[0], hy_b2[0], hy_f2[0],
                           hy_w3[0])
    dft_b = dft.astype(BF16)
    y_d = _hyena(uc.reshape(B, S, -1), hy_bias[0], dft_b, jnp.swapaxes(dft_b, 1, 2), spec)
    wo = cd_w_out[0].astype(BF16)
    x2 = _out_proj(x3, y_c.reshape(T, -1), y_d.reshape(T, -1), wo[:RET_V], wo[RET_V:])
    out = _conv_ffn(x2, S, norm_ffn[1], ffn_w_in[1], ffn_conv_w[1], ffn_conv_b[1],
                    ffn_w_out[1], final_norm, True)
    return out.reshape(B, S, D)
```
